```python
import jax, jax.numpy as jnp
from jax import lax
import numpy as np

D_MODEL = 2048
BATCH = 1
SEQ = 8192
DEPTH = 4

N_MEM = 256
XA_HEADS = 4
XA_HEAD_DIM = D_MODEL // XA_HEADS
GLA_HEADS = 4
GLA_DK = 64
GLA_DV = 128
GLA_GATE_RANK = 16
GLA_GATE_TAU = 16.0
GLA_CHUNK = 64
SWA_Q_HEADS = 16
SWA_KV_HEADS = 2
SWA_HEAD_DIM = 64
SWA_WINDOW = 128
ROPE_THETA = 10000.0
SC_CH = 512
SC_WIDTH = 3
D_MIX = GLA_HEADS * GLA_DV + SWA_Q_HEADS * SWA_HEAD_DIM + SC_CH
IN_SIZES = (
    GLA_HEADS * GLA_DK, GLA_HEADS * GLA_DK, GLA_HEADS * GLA_DV, GLA_HEADS * GLA_DV, GLA_GATE_RANK,
    SWA_Q_HEADS * SWA_HEAD_DIM, SWA_KV_HEADS * SWA_HEAD_DIM, SWA_KV_HEADS * SWA_HEAD_DIM,
    SC_CH, SC_CH, SC_CH,
)
N_IN = sum(IN_SIZES)
D_FF = 5632
FFN_CONV_WIDTH = 3
EPS = 1e-6

kernel_name = "hybrid_gla_swa_shortconv_trunk"


def rms_norm(x, g):
    xf = x.astype(jnp.float32)
    y = xf * lax.rsqrt(jnp.mean(xf * xf, axis=-1, keepdims=True) + EPS)
    return (y * g.astype(jnp.float32)).astype(x.dtype)


def split_cols(z, sizes):
    return jnp.split(z, [int(i) for i in np.cumsum(sizes)[:-1]], axis=-1)


def rope_tables(positions, dim):
    inv = 1.0 / (ROPE_THETA ** (jnp.arange(0, dim, 2, dtype=jnp.float32) / dim))
    ang = positions.astype(jnp.float32)[..., None] * inv
    return jnp.cos(ang), jnp.sin(ang)


def apply_rope(x, cos, sin):
    xf = x.astype(jnp.float32)
    x1, x2 = jnp.split(xf, 2, axis=-1)
    c, s = cos[:, :, None, :], sin[:, :, None, :]
    return jnp.concatenate([x1 * c - x2 * s, x2 * c + x1 * s], axis=-1)


def causal_dwconv(u, w):
    K = w.shape[0]
    T = u.shape[1]
    up = jnp.pad(u, ((0, 0), (K - 1, 0), (0, 0)))
    y = up[:, 0:T] * w[0]
    for j in range(1, K):
        y = y + up[:, j:j + T] * w[j]
    return y


def gla_chunked(q, k, v, log_a):
    B, T, H, dk = q.shape
    dv = v.shape[-1]
    C = GLA_CHUNK
    NC = T // C

    def to_chunks(a):
        return a.astype(jnp.float32).reshape(B, NC, C, H, a.shape[-1]).transpose(1, 0, 3, 2, 4)

    qc, kc, vc, gc = to_chunks(q * (dk ** -0.5)), to_chunks(k), to_chunks(v), to_chunks(log_a)
    mask = jnp.tril(jnp.ones((C, C), dtype=bool))[:, :, None]

    def step(S, inp):
        qi, ki, vi, gi = inp
        b = jnp.cumsum(gi, axis=-2)
        b_last = b[..., -1:, :]
        o_inter = jnp.einsum('bhcd,bhde->bhce', qi * jnp.exp(b), S)
        diff = b[..., :, None, :] - b[..., None, :, :]
        decay = jnp.exp(jnp.where(mask, diff, -jnp.inf))
        att = jnp.einsum('bhtd,bhsd,bhtsd->bhts', qi, ki, decay)
        o_intra = jnp.einsum('bhts,bhse->bhte', att, vi)
        S_new = S * jnp.exp(b_last)[:, :, 0, :, None] + jnp.einsum(
            'bhsd,bhse->bhde', ki * jnp.exp(b_last - b), vi)
        return S_new, o_inter + o_intra

    S0 = jnp.zeros((B, H, dk, dv), jnp.float32)
    _, out = lax.scan(step, S0, (qc, kc, vc, gc))
    return out.transpose(1, 0, 3, 2, 4).reshape(B, T, H, dv)


def swa_with_sinks(q, k, v, sinks):
    B, T, Hq, hd = q.shape
    Hkv = k.shape[2]
    G = Hq // Hkv
    W = SWA_WINDOW
    NB = T // W
    vf = v.astype(jnp.float32)
    qb = q.reshape(B, NB, W, Hkv, G, hd)

    def with_prev(a):
        ab = a.reshape(B, NB, W, Hkv, hd)
        prev = jnp.concatenate([jnp.zeros_like(ab[:, :1]), ab[:, :-1]], axis=1)
        return jnp.concatenate([prev, ab], axis=2)

    kk, vv = with_prev(k), with_prev(vf)
    s = jnp.einsum('bnqhgd,bnkhd->bnhgqk', qb, kk) * (hd ** -0.5)
    qi = jnp.arange(W)[:, None] + W
    ki = jnp.arange(2 * W)[None, :]
    rel = qi - ki
    allowed = (rel >= 0) & (rel < W)
    blk = jnp.arange(NB)[:, None, None]
    valid = allowed[None] & ((blk > 0) | (ki[None] >= W))
    s = jnp.where(valid[None, :, None, None], s, -jnp.inf)
    sink = sinks.astype(jnp.float32).reshape(Hkv, G)[None, None, :, :, None, None]
    m = jnp.maximum(jnp.max(s, axis=-1, keepdims=True), sink)
    p = jnp.exp(s - m)
    denom = jnp.sum(p, axis=-1, keepdims=True) + jnp.exp(sink - m)
    o = jnp.einsum('bnhgqk,bnkhd->bnqhgd', p / denom, vv)
    return o.reshape(B, T, Hq * hd)


def hybrid_mixer(h, cos, sin, w_in, gla_w_gate, gla_b_gate, gla_norm, swa_sinks, sc_conv, w_out):
    B, T, _ = h.shape
    z = h @ w_in
    (g_q, g_k, g_v, g_r, g_lr, s_q, s_k, s_v, c_b, c_c, c_h) = split_cols(z, IN_SIZES)
    log_a = jax.nn.log_sigmoid((g_lr @ gla_w_gate + gla_b_gate).astype(jnp.float32)) / GLA_GATE_TAU
    o_a = gla_chunked(g_q.reshape(B, T, GLA_HEADS, GLA_DK), g_k.reshape(B, T, GLA_HEADS, GLA_DK),
                      g_v.reshape(B, T, GLA_HEADS, GLA_DV), log_a.reshape(B, T, GLA_HEADS, GLA_DK))
    o_a = rms_norm(o_a, gla_norm).reshape(B, T, GLA_HEADS * GLA_DV)
    o_a = (o_a * jax.nn.silu(g_r.astype(jnp.float32))).astype(h.dtype)
    q = apply_rope(s_q.reshape(B, T, SWA_Q_HEADS, SWA_HEAD_DIM), cos, sin)
    k = apply_rope(s_k.reshape(B, T, SWA_KV_HEADS, SWA_HEAD_DIM), cos, sin)
    o_b = swa_with_sinks(q, k, s_v.reshape(B, T, SWA_KV_HEADS, SWA_HEAD_DIM), swa_sinks).astype(h.dtype)
    o_c = c_b * causal_dwconv(c_c * c_h, sc_conv)
    return jnp.concatenate([o_a, o_b, o_c.astype(h.dtype)], axis=-1) @ w_out


def memory_cross_attention(h, m, wq, wk, wv, wo):
    B, T, _ = h.shape
    q = (h @ wq).reshape(B, T, XA_HEADS, XA_HEAD_DIM).astype(jnp.float32)
    k = (m @ wk).reshape(B, N_MEM, XA_HEADS, XA_HEAD_DIM).astype(jnp.float32)
    v = (m @ wv).reshape(B, N_MEM, XA_HEADS, XA_HEAD_DIM).astype(jnp.float32)
    p = jax.nn.softmax(jnp.einsum('bthd,bmhd->bhtm', q, k) * (XA_HEAD_DIM ** -0.5), axis=-1)
    o = jnp.einsum('bhtm,bmhd->bthd', p, v).reshape(B, T, D_MODEL).astype(h.dtype)
    return o @ wo


def conv_ffn(h, w_up, conv_w, conv_b, w_down):
    u = causal_dwconv(h @ w_up, conv_w) + conv_b
    g, val = jnp.split(u, 2, axis=-1)
    return (jax.nn.silu(g) * val) @ w_down


def setup_inputs(seed: int = 0) -> dict:
    key = jax.random.key(seed)
    ks = jax.random.split(key, 24)
    f32 = jnp.float32

    def nrm(k, shape, scale):
        return jax.random.normal(k, shape, f32) * scale

    def gain(k, shape):
        return 1.0 + 0.02 * jax.random.normal(k, shape, f32)

    offset = jax.random.randint(ks[2], (BATCH, 1), 0, 1024, dtype=jnp.int32)
    positions = offset + jnp.arange(SEQ, dtype=jnp.int32)[None, :]
    return {
        "x": nrm(ks[0], (BATCH, SEQ, D_MODEL), 1.0),
        "mem": nrm(ks[1], (BATCH, N_MEM, D_MODEL), 1.0),
        "positions": positions,
        "norm_mix": gain(ks[3], (DEPTH, D_MODEL)),
        "w_in": nrm(ks[4], (DEPTH, D_MODEL, N_IN), D_MODEL ** -0.5),
        "gla_w_gate": nrm(ks[5], (DEPTH, GLA_GATE_RANK, GLA_HEADS * GLA_DK), GLA_GATE_RANK ** -0.5),
        "gla_b_gate": nrm(ks[6], (DEPTH, GLA_HEADS * GLA_DK), 0.1),
        "gla_norm": gain(ks[7], (DEPTH, GLA_DV)),
        "swa_sinks": nrm(ks[8], (DEPTH, SWA_Q_HEADS), 0.5),
        "sc_conv": nrm(ks[9], (DEPTH, SC_WIDTH, SC_CH), SC_WIDTH ** -0.5),
        "w_out": nrm(ks[10], (DEPTH, D_MIX, D_MODEL), D_MIX ** -0.5),
        "norm_x": gain(ks[11], (DEPTH, D_MODEL)),
        "norm_mem": gain(ks[12], (DEPTH, D_MODEL)),
        "xa_wq": nrm(ks[13], (DEPTH, D_MODEL, D_MODEL), D_MODEL ** -0.5),
        "xa_wk": nrm(ks[14], (DEPTH, D_MODEL, D_MODEL), D_MODEL ** -0.5),
        "xa_wv": nrm(ks[15], (DEPTH, D_MODEL, D_MODEL), D_MODEL ** -0.5),
        "xa_wo": nrm(ks[16], (DEPTH, D_MODEL, D_MODEL), D_MODEL ** -0.5),
        "norm_ffn": gain(ks[17], (DEPTH, D_MODEL)),
        "ffn_w_up": nrm(ks[18], (DEPTH, D_MODEL, 2 * D_FF), D_MODEL ** -0.5),
        "ffn_conv": nrm(ks[19], (DEPTH, FFN_CONV_WIDTH, 2 * D_FF), FFN_CONV_WIDTH ** -0.5),
        "ffn_conv_b": nrm(ks[20], (DEPTH, 2 * D_FF), 0.01),
        "ffn_w_down": nrm(ks[21], (DEPTH, D_FF, D_MODEL), D_FF ** -0.5),
        "norm_final": gain(ks[22], (D_MODEL,)),
    }


def reference(x, mem, positions, norm_mix, w_in, gla_w_gate, gla_b_gate, gla_norm, swa_sinks, sc_conv,
              w_out, norm_x, norm_mem, xa_wq, xa_wk, xa_wv, xa_wo, norm_ffn, ffn_w_up, ffn_conv,
              ffn_conv_b, ffn_w_down, norm_final):
    cos, sin = rope_tables(positions, SWA_HEAD_DIM)
    h = x
    for l in range(DEPTH):
        h = h + hybrid_mixer(rms_norm(h, norm_mix[l]), cos, sin, w_in[l], gla_w_gate[l], gla_b_gate[l],
                             gla_norm[l], swa_sinks[l], sc_conv[l], w_out[l])
        h = h + memory_cross_attention(rms_norm(h, norm_x[l]), rms_norm(mem, norm_mem[l]),
                                       xa_wq[l], xa_wk[l], xa_wv[l], xa_wo[l])
        h = h + conv_ffn(rms_norm(h, norm_ffn[l]), ffn_w_up[l], ffn_conv[l], ffn_conv_b[l], ffn_w_down[l])
    return rms_norm(h, norm_final)
```

```python
import functools

import numpy as np
import jax
import jax.numpy as jnp
from jax import lax
from jax.experimental import pallas as pl
from jax.experimental.pallas import tpu as pltpu

F32 = jnp.float32
BF16 = jnp.bfloat16

D_MODEL = 2048
N_MEM = 256
XA_HEADS = 4
XA_HEAD_DIM = D_MODEL // XA_HEADS
GLA_HEADS = 4
GLA_DK = 64
GLA_DV = 128
GLA_GATE_RANK = 16
GLA_GATE_TAU = 16.0
SWA_Q_HEADS = 16
SWA_KV_HEADS = 2
SWA_HEAD_DIM = 64
SWA_WINDOW = 128
ROPE_THETA = 10000.0
SC_CH = 512
SC_WIDTH = 3
D_FF = 5632
EPS = 1e-6

GLA_QK = GLA_HEADS * GLA_DK
GLA_V = GLA_HEADS * GLA_DV
SWA_Q = SWA_Q_HEADS * SWA_HEAD_DIM
SWA_KV = SWA_KV_HEADS * SWA_HEAD_DIM
SWA_GROUP = SWA_Q // SWA_KV_HEADS

LANES = 128
BF16_ROWS = 16
VMEM_LIMIT = 52 * 1024 * 1024

Z_Q, Z_K, Z_V, Z_R = 0, 256, 512, 1024
Z_SQ, Z_CB, Z_CC, Z_CH = 1536, 2560, 3072, 3584
Z_SK, Z_SV, Z_LR = 4096, 4224, 4352
Z_COLS = 4480

GLA_CHUNK = 64
GLA_SUB = 16
GLA_NSUB = GLA_CHUNK // GLA_SUB
NEG_BIG = -1e30


def _params(*sem):
    return pltpu.CompilerParams(dimension_semantics=sem, vmem_limit_bytes=VMEM_LIMIT)


def _dot(a, b):
    return jnp.dot(a, b, preferred_element_type=F32)


def _dot_nt(a, b):
    return lax.dot_general(a, b, (((1,), (1,)), ((), ())), preferred_element_type=F32)


def _dot_tn(a, b):
    return lax.dot_general(a, b, (((0,), (0,)), ((), ())), preferred_element_type=F32)


def _silu(x):
    return x / (1.0 + jnp.exp(-x))


def _rmsnorm_body(x_ref, g_ref, o_ref):
    x = x_ref[...]
    inv = lax.rsqrt(jnp.mean(x * x, axis=-1, keepdims=True) + EPS)
    o_ref[...] = (x * inv * g_ref[...]).astype(o_ref.dtype)


def _rmsnorm(x, gain, out_dtype, tm):
    m, d = x.shape
    tm = min(tm, m)
    return pl.pallas_call(
        _rmsnorm_body,
        grid=(m // tm,),
        in_specs=[pl.BlockSpec((tm, d), lambda i: (i, 0)),
                  pl.BlockSpec((1, d), lambda i: (0, 0))],
        out_specs=pl.BlockSpec((tm, d), lambda i: (i, 0)),
        out_shape=jax.ShapeDtypeStruct((m, d), out_dtype),
        compiler_params=_params("parallel"),
        name="rmsnorm",
    )(x, gain.reshape(1, d))


def _matmul_body(a_ref, w_ref, o_ref):
    o_ref[...] = _dot(a_ref[...], w_ref[...]).astype(o_ref.dtype)


def _matmul(a, w, tm, tn, name):
    m, k = a.shape
    n = w.shape[1]
    tm = min(tm, m)
    return pl.pallas_call(
        _matmul_body,
        grid=(m // tm, n // tn),
        in_specs=[pl.BlockSpec((tm, k), lambda i, j: (i, 0)),
                  pl.BlockSpec((k, tn), lambda i, j: (0, j))],
        out_specs=pl.BlockSpec((tm, tn), lambda i, j: (i, j)),
        out_shape=jax.ShapeDtypeStruct((m, n), BF16),
        compiler_params=_params("parallel", "arbitrary"),
        name=name,
    )(a, w)


def _resid_body(*refs, starts, counts, nk, emit_h):
    npieces = len(starts)
    a_refs = refs[:npieces]
    w_ref, h_ref, g_ref = refs[npieces:npieces + 3]
    outs = refs[npieces + 3:]
    if emit_h:
        hout_ref, hn_ref, acc_ref = outs
    else:
        hn_ref, acc_ref = outs
    k = pl.program_id(1)

    @pl.when(k == 0)
    def _():
        acc_ref[...] = h_ref[...]

    for a_ref, s, c in zip(a_refs, starts, counts):
        @pl.when((k >= s) & (k < s + c))
        def _(a_ref=a_ref):
            acc_ref[...] += _dot(a_ref[...], w_ref[...])

    @pl.when(k == nk - 1)
    def _():
        h = acc_ref[...]
        if emit_h:
            hout_ref[...] = h
        inv = lax.rsqrt(jnp.mean(h * h, axis=-1, keepdims=True) + EPS)
        hn_ref[...] = (h * inv * g_ref[...]).astype(hn_ref.dtype)


def _resid_norm(pieces, w, h, gain, *, tm, tk, emit_h, hn_dtype, name):
    m, d = h.shape
    tm = min(tm, m)
    counts = [p.shape[1] // tk for p in pieces]
    starts = [int(s) for s in np.cumsum([0] + counts[:-1])]
    nk = sum(counts)
    assert nk * tk == w.shape[0]

    def piece_spec(s, c):
        return pl.BlockSpec((tm, tk), lambda i, k: (i, jnp.clip(k - s, 0, c - 1)))

    in_specs = [piece_spec(s, c) for s, c in zip(starts, counts)]
    in_specs += [pl.BlockSpec((tk, d), lambda i, k: (k, 0)),
                 pl.BlockSpec((tm, d), lambda i, k: (i, 0)),
                 pl.BlockSpec((1, d), lambda i, k: (0, 0))]
    row_spec = pl.BlockSpec((tm, d), lambda i, k: (i, 0))
    out_specs = [row_spec]
    out_shape = [jax.ShapeDtypeStruct((m, d), hn_dtype)]
    if emit_h:
        out_specs = [row_spec, row_spec]
        out_shape = [jax.ShapeDtypeStruct((m, d), F32)] + out_shape
    res = pl.pallas_call(
        functools.partial(_resid_body, starts=starts, counts=counts, nk=nk, emit_h=emit_h),
        grid=(m // tm, nk),
        in_specs=in_specs,
        out_specs=out_specs,
        out_shape=out_shape,
        scratch_shapes=[pltpu.VMEM((tm, d), F32)],
        compiler_params=_params("parallel", "arbitrary"),
        name=name,
    )(*pieces, w, h, gain.reshape(1, d))
    return res if emit_h else (None, res[0])


def _xa_body(a_ref, wq_ref, k_ref, v_ref, o_ref):
    q = _dot(a_ref[...], wq_ref[...]).astype(BF16)
    s = _dot_nt(q, k_ref[...]) * (XA_HEAD_DIM ** -0.5)
    m = jnp.max(s, axis=-1, keepdims=True)
    p = jnp.exp(s - m)
    denom = jnp.sum(p, axis=-1, keepdims=True)
    o = _dot(p.astype(BF16), v_ref[...]) / denom
    o_ref[...] = o.astype(o_ref.dtype)


def _cross_attention(hn, wq, mem_k, mem_v, tm):
    m, d = hn.shape
    tm = min(tm, m)
    hd = XA_HEAD_DIM
    return pl.pallas_call(
        _xa_body,
        grid=(m // tm, XA_HEADS),
        in_specs=[pl.BlockSpec((tm, d), lambda i, j: (i, 0)),
                  pl.BlockSpec((d, hd), lambda i, j: (0, j)),
                  pl.BlockSpec((N_MEM, hd), lambda i, j: (0, j)),
                  pl.BlockSpec((N_MEM, hd), lambda i, j: (0, j))],
        out_specs=pl.BlockSpec((tm, hd), lambda i, j: (i, j)),
        out_shape=jax.ShapeDtypeStruct((m, d), BF16),
        compiler_params=_params("parallel", "arbitrary"),
        name="cross_attention",
    )(hn, wq, mem_k, mem_v)


def _ffn_up_body(a_ref, halo_ref, wg_ref, wv_ref, cg_ref, cv_ref, bg_ref, bv_ref, o_ref):
    tm = a_ref.shape[0]
    a = a_ref[...]
    halo = jnp.where(pl.program_id(0) > 0, halo_ref[...], jnp.zeros_like(halo_ref))

    def conv_branch(w_ref, c_ref, b_ref):
        w = w_ref[...]
        u = _dot(a, w)
        uh = _dot(halo, w)
        ext = jnp.concatenate([uh[8:], u], axis=0)
        c = c_ref[...]
        return (c[2:3] * u + c[1:2] * ext[7:7 + tm] + c[0:1] * ext[6:6 + tm]) + b_ref[...]

    g = conv_branch(wg_ref, cg_ref, bg_ref)
    val = conv_branch(wv_ref, cv_ref, bv_ref)
    o_ref[...] = (_silu(g) * val).astype(o_ref.dtype)


def _ffn_up(hn, w_up, conv_w, conv_b, tm, tn):
    m, d = hn.shape
    tm = min(tm, m)
    nj = D_FF // tn
    halo_blocks = tm // BF16_ROWS
    return pl.pallas_call(
        _ffn_up_body,
        grid=(m // tm, nj),
        in_specs=[pl.BlockSpec((tm, d), lambda i, j: (i, 0)),
                  pl.BlockSpec((BF16_ROWS, d), lambda i, j: (jnp.maximum(i * halo_blocks - 1, 0), 0)),
                  pl.BlockSpec((d, tn), lambda i, j: (0, j)),
                  pl.BlockSpec((d, tn), lambda i, j: (0, nj + j)),
                  pl.BlockSpec((SC_WIDTH, tn), lambda i, j: (0, j)),
                  pl.BlockSpec((SC_WIDTH, tn), lambda i, j: (0, nj + j)),
                  pl.BlockSpec((1, tn), lambda i, j: (0, j)),
                  pl.BlockSpec((1, tn), lambda i, j: (0, nj + j))],
        out_specs=pl.BlockSpec((tm, tn), lambda i, j: (i, j)),
        out_shape=jax.ShapeDtypeStruct((m, D_FF), BF16),
        compiler_params=_params("parallel", "arbitrary"),
        name="ffn_up",
    )(hn, hn, w_up, w_up, conv_w, conv_w, conv_b.reshape(1, -1), conv_b.reshape(1, -1))


def _swa_body(sink_ref, q0_ref, q1_ref, kc_ref, kp_ref, vc_ref, vp_ref,
              cosc_ref, sinc_ref, cosp_ref, sinp_ref,
              cb_ref, cc_ref, ch_ref, hcc_ref, hch_ref, cw_ref, ob_ref, oc_ref):
    w = SWA_WINDOW
    n = pl.program_id(0)
    lane = lax.broadcasted_iota(jnp.int32, (w, LANES), 1)
    low_half = lane < SWA_HEAD_DIM
    first_rot = (lane & (SWA_HEAD_DIM - 1)) < SWA_HEAD_DIM // 2

    def rope(x, cos, sin):
        half = SWA_HEAD_DIM // 2
        swapped = jnp.where(first_rot, pltpu.roll(x, LANES - half, 1), pltpu.roll(x, half, 1))
        return x * cos + swapped * sin

    cosc, sinc = cosc_ref[...], sinc_ref[...]
    k_all = jnp.concatenate([rope(kp_ref[...].astype(F32), cosp_ref[...], sinp_ref[...]),
                             rope(kc_ref[...].astype(F32), cosc, sinc)], axis=0)
    v_all = jnp.concatenate([vp_ref[...], vc_ref[...]], axis=0).astype(F32)
    low2 = jnp.concatenate([low_half, low_half], axis=0)
    k_rot = pltpu.roll(k_all, SWA_HEAD_DIM, 1)
    v_rot = pltpu.roll(v_all, SWA_HEAD_DIM, 1)

    qi = lax.broadcasted_iota(jnp.int32, (w, 2 * w), 0)
    ki = lax.broadcasted_iota(jnp.int32, (w, 2 * w), 1)
    allowed = (ki > qi) & (ki <= qi + w) & ((n > 0) | (ki >= w))
    bias = jnp.where(allowed, 0.0, NEG_BIG)
    bias2 = jnp.concatenate([bias, bias], axis=0)
    row2 = lax.broadcasted_iota(jnp.int32, (2 * w, 1), 0)

    pairs = SWA_GROUP // LANES
    for g, q_ref in enumerate((q0_ref, q1_ref)):
        if g == 0:
            kg = jnp.where(low2, k_all, k_rot).astype(BF16)
            vg = jnp.where(low2, v_all, v_rot).astype(BF16)
        else:
            kg = jnp.where(low2, k_rot, k_all).astype(BF16)
            vg = jnp.where(low2, v_rot, v_all).astype(BF16)
        for p in range(pairs):
            qp = rope(q_ref[:, p * LANES:(p + 1) * LANES].astype(F32), cosc, sinc) * (SWA_HEAD_DIM ** -0.5)
            stack = jnp.concatenate([jnp.where(low_half, qp, 0.0), jnp.where(low_half, 0.0, qp)], axis=0)
            s = _dot_nt(stack.astype(BF16), kg) + bias2
            head = g * (SWA_Q_HEADS // SWA_KV_HEADS) + 2 * p
            sink = jnp.where(row2 < w, sink_ref[head], sink_ref[head + 1])
            mx = jnp.maximum(jnp.max(s, axis=-1, keepdims=True), sink)
            pr = jnp.exp(s - mx)
            denom = jnp.sum(pr, axis=-1, keepdims=True) + jnp.exp(sink - mx)
            o = _dot(pr.astype(BF16), vg) / denom
            col = (g * pairs + p) * LANES
            ob_ref[:, col:col + LANES] = jnp.where(low_half, o[:w], o[w:]).astype(ob_ref.dtype)

    u = cc_ref[...].astype(F32) * ch_ref[...].astype(F32)
    uh = hcc_ref[...].astype(F32) * hch_ref[...].astype(F32)
    uh = jnp.where(n > 0, uh, 0.0)
    ext = jnp.concatenate([uh[8:], u], axis=0)
    cw = cw_ref[...]
    y = cw[2:3] * u + cw[1:2] * ext[7:7 + w] + cw[0:1] * ext[6:6 + w]
    oc_ref[...] = (cb_ref[...].astype(F32) * y).astype(oc_ref.dtype)


def _swa_conv(z, cos_t, sin_t, sinks, conv_w):
    t = z.shape[0]
    w = SWA_WINDOW
    nb = t // w

    def cur(width, off):
        return pl.BlockSpec((w, width), lambda i: (i, off // width))

    def prev(width, off):
        return pl.BlockSpec((w, width), lambda i: (jnp.maximum(i - 1, 0), off // width))

    def halo(off):
        return pl.BlockSpec((BF16_ROWS, SC_CH),
                            lambda i: (jnp.maximum(i * (w // BF16_ROWS) - 1, 0), off // SC_CH))

    tab_c = pl.BlockSpec((w, LANES), lambda i: (i, 0))
    tab_p = pl.BlockSpec((w, LANES), lambda i: (jnp.maximum(i - 1, 0), 0))
    return pl.pallas_call(
        _swa_body,
        grid=(nb,),
        in_specs=[pl.BlockSpec(memory_space=pltpu.SMEM),
                  cur(SWA_GROUP, Z_SQ), cur(SWA_GROUP, Z_SQ + SWA_GROUP),
                  cur(SWA_KV, Z_SK), prev(SWA_KV, Z_SK), cur(SWA_KV, Z_SV), prev(SWA_KV, Z_SV),
                  tab_c, tab_c, tab_p, tab_p,
                  cur(SC_CH, Z_CB), cur(SC_CH, Z_CC), cur(SC_CH, Z_CH), halo(Z_CC), halo(Z_CH),
                  pl.BlockSpec((SC_WIDTH, SC_CH), lambda i: (0, 0))],
        out_specs=[pl.BlockSpec((w, SWA_Q), lambda i: (i, 0)),
                   pl.BlockSpec((w, SC_CH), lambda i: (i, 0))],
        out_shape=[jax.ShapeDtypeStruct((t, SWA_Q), BF16),
                   jax.ShapeDtypeStruct((t, SC_CH), BF16)],
        compiler_params=_params("parallel"),
        name="swa_conv",
    )(sinks, z, z, z, z, z, z, cos_t, sin_t, cos_t, sin_t, z, z, z, z, z, conv_w)


def _gla_constants():
    c, r = GLA_CHUNK, GLA_SUB
    t = np.arange(c)[:, None]
    s = np.arange(c)[None, :]
    blk_t, blk_s = t // r, s // r
    mats = [
        (s <= t),
        (s > t),
        (blk_s == blk_t) & (s <= t) & (s > blk_t * r),
        (blk_s == blk_t) & (s > t),
    ]
    for i in range(1, GLA_NSUB):
        mats.append((blk_t < i) & (s > blk_t * r + r - 1) & (s <= i * r))
    sel = np.concatenate(mats, axis=0).astype(np.float32)
    d_head = np.arange(GLA_QK)[:, None] // GLA_DK
    e_head = np.arange(GLA_V)[None, :] // GLA_DV
    expand = (d_head == e_head).astype(np.float32)
    return jnp.asarray(sel, BF16), jnp.asarray(expand, BF16), jnp.asarray(expand.T, F32)


def _gla_body(q_ref, k_ref, v_ref, r_ref, lr_ref, wg_ref, bg_ref, gn_ref, sel_ref, ex_ref, bd_ref,
              o_ref, st_ref, la_ref):
    c, r = GLA_CHUNK, GLA_SUB
    tc = q_ref.shape[0]

    @pl.when(pl.program_id(0) == 0)
    def _():
        st_ref[...] = jnp.zeros_like(st_ref)

    pre = _dot(lr_ref[...], wg_ref[...]) + bg_ref[...]
    la_ref[...] = (jnp.minimum(pre, 0.0) - jnp.log(1.0 + jnp.exp(-jnp.abs(pre)))) * (1.0 / GLA_GATE_TAU)

    lane_qk = lax.broadcasted_iota(jnp.int32, (r, GLA_QK), 1)
    lane_v = lax.broadcasted_iota(jnp.int32, (r, GLA_V), 1)
    row_sub = lax.broadcasted_iota(jnp.int32, (r, GLA_QK), 0)
    col_chunk = lax.broadcasted_iota(jnp.int32, (c, c), 1)

    def head_rows(x, lane, width):
        return jnp.concatenate(
            [jnp.where((lane >= h * width) & (lane < (h + 1) * width), x, 0.0) for h in range(GLA_HEADS)],
            axis=0)

    def head_diag(x, lane, width):
        out = jnp.where(lane < width, x[0:r], 0.0)
        for h in range(1, GLA_HEADS):
            out = out + jnp.where((lane >= h * width) & (lane < (h + 1) * width), x[h * r:(h + 1) * r], 0.0)
        return out

    def chunk(ci, carry):
        r0 = pl.multiple_of(ci * c, c)
        rows = pl.ds(r0, c)
        g = la_ref[rows, :]
        g_hi = g.astype(BF16)
        g_lo = (g - g_hi.astype(F32)).astype(BF16)
        cs = _dot(sel_ref[...], jnp.concatenate([g_hi, g_lo], axis=1))
        cs = cs[:, :GLA_QK] + cs[:, GLA_QK:]
        e_b = jnp.exp(cs[0:c])
        e_tail = jnp.exp(cs[c:2 * c])
        bw = cs[2 * c:3 * c]
        e_q = jnp.exp(bw)
        e_k = jnp.exp(cs[3 * c:4 * c])

        q = q_ref[rows, :].astype(F32) * (GLA_DK ** -0.5)
        k = k_ref[rows, :].astype(F32)
        v = v_ref[rows, :]
        vf = v.astype(F32)

        st = st_ref[...]
        o_inter = _dot_nt((q * e_b).astype(BF16), st.astype(BF16))
        upd = _dot_tn(v, (k * e_tail).astype(BF16))
        st_ref[...] = st * e_b[c - 1:c, :] + upd * bd_ref[...]

        qs = q * e_q
        ks = k * e_k
        atts = []
        for i in range(1, GLA_NSUB):
            ki = (ks * jnp.exp(cs[(3 + i) * c:(4 + i) * c])).astype(BF16)
            qst = head_rows(qs[i * r:(i + 1) * r], lane_qk, GLA_DK).astype(BF16)
            att = _dot_nt(qst, ki)
            atts.append(jnp.where(col_chunk < i * r, att, 0.0))
        ov = _dot(jnp.concatenate(atts, axis=0).astype(BF16), v)

        outs = []
        for i in range(GLA_NSUB):
            sub = slice(i * r, (i + 1) * r)
            qi, ki, bwi, vi = q[sub], k[sub], bw[sub], vf[sub]
            ps = []
            for s in range(r):
                e = jnp.exp(jnp.where(row_sub >= s, bwi - bwi[s:s + 1, :], NEG_BIG))
                ps.append(((qi * ki[s:s + 1, :]) * e).astype(BF16))
            rep = _dot(jnp.concatenate(ps, axis=0), ex_ref[...])
            o_i = o_inter[sub]
            for s in range(r):
                o_i = o_i + rep[s * r:(s + 1) * r] * vi[s:s + 1, :]
            if i > 0:
                o_i = o_i + head_diag(ov[(i - 1) * GLA_HEADS * r:i * GLA_HEADS * r], lane_v, GLA_DV)
            outs.append(o_i)
        o = jnp.concatenate(outs, axis=0)

        normed = []
        for h in range(GLA_HEADS):
            oh = o[:, h * GLA_DV:(h + 1) * GLA_DV]
            normed.append(oh * lax.rsqrt(jnp.mean(oh * oh, axis=-1, keepdims=True) + EPS))
        y = jnp.concatenate(normed, axis=1) * gn_ref[...]
        o_ref[rows, :] = (y * _silu(r_ref[rows, :].astype(F32))).astype(o_ref.dtype)
        return carry

    lax.fori_loop(0, tc // c, chunk, 0)


def _gla(z, w_gate, b_gate, gain, tc):
    t = z.shape[0]
    tc = min(tc, t)
    sel, expand, bd_mask = _gla_constants()

    def col(width, off):
        return pl.BlockSpec((tc, width), lambda i: (i, off // width))

    def whole(a):
        return pl.BlockSpec(a.shape, lambda i: (0, 0))

    gain_t = jnp.tile(gain.reshape(1, GLA_DV), (1, GLA_HEADS))
    b_gate = b_gate.reshape(1, GLA_QK)
    return pl.pallas_call(
        _gla_body,
        grid=(t // tc,),
        in_specs=[col(GLA_QK, Z_Q), col(GLA_QK, Z_K), col(GLA_V, Z_V), col(GLA_V, Z_R), col(LANES, Z_LR),
                  whole(w_gate), whole(b_gate), whole(gain_t), whole(sel), whole(expand), whole(bd_mask)],
        out_specs=pl.BlockSpec((tc, GLA_V), lambda i: (i, 0)),
        out_shape=jax.ShapeDtypeStruct((t, GLA_V), BF16),
        scratch_shapes=[pltpu.VMEM((GLA_V, GLA_QK), F32), pltpu.VMEM((tc, GLA_QK), F32)],
        compiler_params=_params("arbitrary"),
        name="gla",
    )(z, z, z, z, z, w_gate, b_gate, gain_t, sel, expand, bd_mask)


def _rope_tables(positions):
    half = SWA_HEAD_DIM // 2
    inv = 1.0 / (ROPE_THETA ** (jnp.arange(0, SWA_HEAD_DIM, 2, dtype=F32) / SWA_HEAD_DIM))
    ang = positions.astype(F32)[:, None] * inv
    cos, sin = jnp.cos(ang), jnp.sin(ang)
    reps = LANES // SWA_HEAD_DIM
    return (jnp.tile(jnp.concatenate([cos, cos], axis=-1), (1, reps)),
            jnp.tile(jnp.concatenate([-sin, sin], axis=-1), (1, reps)))


def _reorder_w_in(w_in):
    o = np.cumsum([0, GLA_QK, GLA_QK, GLA_V, GLA_V, GLA_GATE_RANK, SWA_Q, SWA_KV, SWA_KV, SC_CH, SC_CH, SC_CH])
    pad = jnp.zeros(w_in.shape[:2] + (LANES - GLA_GATE_RANK,), w_in.dtype)
    return jnp.concatenate([w_in[..., o[0]:o[4]], w_in[..., o[5]:o[6]], w_in[..., o[8]:o[11]],
                            w_in[..., o[6]:o[8]], w_in[..., o[4]:o[5]], pad], axis=-1).astype(BF16)


def kernel(x, mem, positions, norm_mix, w_in, gla_w_gate, gla_b_gate, gla_norm, swa_sinks, sc_conv, w_out,
           norm_x, norm_mem, xa_wq, xa_wk, xa_wv, xa_wo, norm_ffn, ffn_w_up, ffn_conv, ffn_conv_b, ffn_w_down,
           norm_final):
    assert x.shape[0] == 1 and mem.shape[0] == 1
    depth = w_in.shape[0]
    h = x[0]
    m = mem[0]
    cos_t, sin_t = _rope_tables(positions[0])

    w_in_z = _reorder_w_in(w_in)
    w_gate = jnp.pad(gla_w_gate, ((0, 0), (0, LANES - GLA_GATE_RANK), (0, 0))).astype(BF16)
    w_out, xa_wq, xa_wk, xa_wv, xa_wo, ffn_w_up, ffn_w_down = (
        a.astype(BF16) for a in (w_out, xa_wq, xa_wk, xa_wv, xa_wo, ffn_w_up, ffn_w_down))

    hn = _rmsnorm(h, norm_mix[0], BF16, 512)
    out = None
    for l in range(depth):
        z = _matmul(hn, w_in_z[l], 1024, 640, "in_proj")
        o_a = _gla(z, w_gate[l], gla_b_gate[l], gla_norm[l], 256)
        o_b, o_c = _swa_conv(z, cos_t, sin_t, swa_sinks[l], sc_conv[l])
        h, hn = _resid_norm([o_a, o_b, o_c], w_out[l], h, norm_x[l], tm=512, tk=512, emit_h=True,
                            hn_dtype=BF16, name="out_proj")
        mem_n = _rmsnorm(m, norm_mem[l], BF16, N_MEM)
        mem_k = _matmul(mem_n, xa_wk[l], N_MEM, 512, "mem_k")
        mem_v = _matmul(mem_n, xa_wv[l], N_MEM, 512, "mem_v")
        o_x = _cross_attention(hn, xa_wq[l], mem_k, mem_v, 1024)
        h, hn = _resid_norm([o_x], xa_wo[l], h, norm_ffn[l], tm=512, tk=512, emit_h=True,
                            hn_dtype=BF16, name="xa_out_proj")
        act = _ffn_up(hn, ffn_w_up[l], ffn_conv[l], ffn_conv_b[l], 1024, 512)
        if l + 1 < depth:
            h, hn = _resid_norm([act], ffn_w_down[l], h, norm_mix[l + 1], tm=512, tk=512, emit_h=True,
                                hn_dtype=BF16, name="ffn_down")
        else:
            _, out = _resid_norm([act], ffn_w_down[l], h, norm_final, tm=512, tk=512, emit_h=False,
                                 hn_dtype=F32, name="ffn_down_final")
    return out[None]
```

```python
import functools

import numpy as np
import jax
import jax.numpy as jnp
from jax import lax
from jax.experimental import pallas as pl
from jax.experimental.pallas import tpu as pltpu

F32 = jnp.float32
BF16 = jnp.bfloat16

D_MODEL = 2048
N_MEM = 256
XA_HEADS = 4
XA_HEAD_DIM = D_MODEL // XA_HEADS
GLA_HEADS = 4
GLA_DK = 64
GLA_DV = 128
GLA_GATE_RANK = 16
GLA_GATE_TAU = 16.0
SWA_Q_HEADS = 16
SWA_KV_HEADS = 2
SWA_HEAD_DIM = 64
SWA_WINDOW = 128
ROPE_THETA = 10000.0
SC_CH = 512
SC_WIDTH = 3
D_FF = 5632
EPS = 1e-6

GLA_QK = GLA_HEADS * GLA_DK
GLA_V = GLA_HEADS * GLA_DV
SWA_Q = SWA_Q_HEADS * SWA_HEAD_DIM
SWA_KV = SWA_KV_HEADS * SWA_HEAD_DIM
SWA_GROUP = SWA_Q // SWA_KV_HEADS
IN_SIZES = (GLA_QK, GLA_QK, GLA_V, GLA_V, GLA_GATE_RANK, SWA_Q, SWA_KV, SWA_KV, SC_CH, SC_CH, SC_CH)
N_IN = sum(IN_SIZES)

LANES = 128
BF16_ROWS = 16
VMEM_LIMIT = 56 * 1024 * 1024

Z_Q, Z_K, Z_V, Z_R = 0, 256, 512, 1024
Z_SQ, Z_CB, Z_CC, Z_CH = 1536, 2560, 3072, 3584
Z_SK, Z_SV, Z_LR = 4096, 4224, 4352
Z_COLS = 4480

GLA_CHUNK = 64
GLA_SUB = 16
GLA_NSUB = GLA_CHUNK // GLA_SUB
NEG_BIG = -1e30


def _params(*sem):
    return pltpu.CompilerParams(dimension_semantics=sem, vmem_limit_bytes=VMEM_LIMIT)


def _dot(a, b):
    return jnp.dot(a, b, preferred_element_type=F32)


def _dot_nt(a, b):
    return lax.dot_general(a, b, (((1,), (1,)), ((), ())), preferred_element_type=F32)


def _dot_tn(a, b):
    return lax.dot_general(a, b, (((0,), (0,)), ((), ())), preferred_element_type=F32)


def _silu(x):
    return x / (1.0 + jnp.exp(-x))


def _normalize(x, gain):
    return x * lax.rsqrt(jnp.mean(x * x, axis=-1, keepdims=True) + EPS) * gain


def _layer_cols(layer, k, tn, col_of):
    return pl.BlockSpec((None, k, tn), lambda *ids: (layer, 0, col_of(*ids)))


def _rmsnorm_body(x_ref, g_ref, o_ref):
    o_ref[...] = _normalize(x_ref[...], g_ref[...]).astype(o_ref.dtype)


def _rmsnorm(x, gain, out_dtype, tm):
    m, d = x.shape
    tm = min(tm, m)
    return pl.pallas_call(
        _rmsnorm_body,
        grid=(m // tm,),
        in_specs=[pl.BlockSpec((tm, d), lambda i: (i, 0)),
                  pl.BlockSpec((1, d), lambda i: (0, 0))],
        out_specs=pl.BlockSpec((tm, d), lambda i: (i, 0)),
        out_shape=jax.ShapeDtypeStruct((m, d), out_dtype),
        compiler_params=_params("parallel"),
        name="rmsnorm",
    )(x, gain.reshape(1, d))


def _reorder_body(w_ref, o_ref):
    o = np.cumsum((0,) + IN_SIZES)
    w = w_ref[...]
    pad = jnp.zeros((w.shape[0], LANES - GLA_GATE_RANK), w.dtype)
    o_ref[...] = jnp.concatenate(
        [w[:, o[0]:o[4]], w[:, o[5]:o[6]], w[:, o[8]:o[11]], w[:, o[6]:o[8]], w[:, o[4]:o[5]], pad],
        axis=1).astype(o_ref.dtype)


def _reorder_w_in(w_in, tk):
    depth, d, n = w_in.shape
    return pl.pallas_call(
        _reorder_body,
        grid=(depth, d // tk),
        in_specs=[pl.BlockSpec((None, tk, n), lambda l, i: (l, i, 0))],
        out_specs=pl.BlockSpec((None, tk, Z_COLS), lambda l, i: (l, i, 0)),
        out_shape=jax.ShapeDtypeStruct((depth, d, Z_COLS), BF16),
        compiler_params=_params("parallel", "parallel"),
        name="reorder_w_in",
    )(w_in)


def _norm_matmul_body(h_ref, g_ref, w_ref, o_ref, hn_ref):
    @pl.when(pl.program_id(1) == 0)
    def _():
        hn_ref[...] = _normalize(h_ref[...], g_ref[...]).astype(BF16)

    o_ref[...] = _dot(hn_ref[...], w_ref[...].astype(BF16)).astype(o_ref.dtype)


def _norm_matmul(h, gain, w, layer, tm, tn, name):
    m, d = h.shape
    n = w.shape[2]
    tm = min(tm, m)
    return pl.pallas_call(
        _norm_matmul_body,
        grid=(m // tm, n // tn),
        in_specs=[pl.BlockSpec((tm, d), lambda i, j: (i, 0)),
                  pl.BlockSpec((1, d), lambda i, j: (0, 0)),
                  _layer_cols(layer, d, tn, lambda i, j: j)],
        out_specs=pl.BlockSpec((tm, tn), lambda i, j: (i, j)),
        out_shape=jax.ShapeDtypeStruct((m, n), BF16),
        scratch_shapes=[pltpu.VMEM((tm, d), BF16)],
        compiler_params=_params("parallel", "arbitrary"),
        name=name,
    )(h, gain.reshape(1, d), w)


def _resid_body(*refs, npieces):
    a_refs = refs[:npieces]
    w_ref, h_ref, o_ref, wb_ref = refs[npieces:]

    @pl.when(pl.program_id(1) == 0)
    def _():
        wb_ref[...] = w_ref[...].astype(BF16)

    if npieces == 1:
        a = a_refs[0][...]
    else:
        a = jnp.concatenate([r[...] for r in a_refs], axis=1)
    o_ref[...] = h_ref[...] + _dot(a, wb_ref[...])


def _matmul_resid(pieces, w, layer, h, tm, tn, name):
    m, d = h.shape
    k = w.shape[1]
    tm = min(tm, m)
    assert sum(p.shape[1] for p in pieces) == k
    in_specs = [pl.BlockSpec((tm, p.shape[1]), lambda j, i: (i, 0)) for p in pieces]
    in_specs += [_layer_cols(layer, k, tn, lambda j, i: j),
                 pl.BlockSpec((tm, tn), lambda j, i: (i, j))]
    return pl.pallas_call(
        functools.partial(_resid_body, npieces=len(pieces)),
        grid=(d // tn, m // tm),
        in_specs=in_specs,
        out_specs=pl.BlockSpec((tm, tn), lambda j, i: (i, j)),
        out_shape=jax.ShapeDtypeStruct((m, d), F32),
        scratch_shapes=[pltpu.VMEM((k, tn), BF16)],
        compiler_params=_params("parallel", "arbitrary"),
        name=name,
    )(*pieces, w, h)


def _xa_body(h_ref, g_ref, wq_ref, k_ref, v_ref, o_ref, hn_ref):
    @pl.when(pl.program_id(1) == 0)
    def _():
        hn_ref[...] = _normalize(h_ref[...], g_ref[...]).astype(BF16)

    q = _dot(hn_ref[...], wq_ref[...].astype(BF16)).astype(BF16)
    s = _dot_nt(q, k_ref[...]) * (XA_HEAD_DIM ** -0.5)
    m = jnp.max(s, axis=-1, keepdims=True)
    p = jnp.exp(s - m)
    denom = jnp.sum(p, axis=-1, keepdims=True)
    o = _dot(p.astype(BF16), v_ref[...]) / denom
    o_ref[...] = o.astype(o_ref.dtype)


def _cross_attention(h, gain, wq, layer, mem_k, mem_v, tm):
    m, d = h.shape
    tm = min(tm, m)
    hd = XA_HEAD_DIM
    return pl.pallas_call(
        _xa_body,
        grid=(m // tm, XA_HEADS),
        in_specs=[pl.BlockSpec((tm, d), lambda i, j: (i, 0)),
                  pl.BlockSpec((1, d), lambda i, j: (0, 0)),
                  _layer_cols(layer, d, hd, lambda i, j: j),
                  pl.BlockSpec((N_MEM, hd), lambda i, j: (0, j)),
                  pl.BlockSpec((N_MEM, hd), lambda i, j: (0, j))],
        out_specs=pl.BlockSpec((tm, hd), lambda i, j: (i, j)),
        out_shape=jax.ShapeDtypeStruct((m, d), BF16),
        scratch_shapes=[pltpu.VMEM((tm, d), BF16)],
        compiler_params=_params("parallel", "arbitrary"),
        name="cross_attention",
    )(h, gain.reshape(1, d), wq, mem_k, mem_v)


def _ffn_up_body(h_ref, halo_ref, g_ref, wg_ref, wv_ref, cg_ref, cv_ref, bg_ref, bv_ref, o_ref, hn_ref):
    tm = h_ref.shape[0]

    @pl.when(pl.program_id(1) == 0)
    def _():
        gain = g_ref[...]
        halo = jnp.where(pl.program_id(0) > 0, _normalize(halo_ref[...], gain), 0.0)
        hn_ref[0:BF16_ROWS, :] = halo.astype(BF16)
        hn_ref[BF16_ROWS:, :] = _normalize(h_ref[...], gain).astype(BF16)

    a = hn_ref[BF16_ROWS:, :]
    halo = hn_ref[0:BF16_ROWS, :]

    def conv_branch(w_ref, c_ref, b_ref):
        w = w_ref[...].astype(BF16)
        u = _dot(a, w)
        uh = _dot(halo, w)
        ext = jnp.concatenate([uh[8:], u], axis=0)
        c = c_ref[...]
        return (c[2:3] * u + c[1:2] * ext[7:7 + tm] + c[0:1] * ext[6:6 + tm]) + b_ref[...]

    g = conv_branch(wg_ref, cg_ref, bg_ref)
    val = conv_branch(wv_ref, cv_ref, bv_ref)
    o_ref[...] = (_silu(g) * val).astype(o_ref.dtype)


def _ffn_up(h, gain, w_up, conv_w, conv_b, layer, tm, tn):
    m, d = h.shape
    tm = min(tm, m)
    nj = D_FF // tn
    halo_blocks = tm // BF16_ROWS

    def vec(rows, col_of):
        return pl.BlockSpec((None, rows, tn), lambda i, j: (layer, 0, col_of(j)))

    return pl.pallas_call(
        _ffn_up_body,
        grid=(m // tm, nj),
        in_specs=[pl.BlockSpec((tm, d), lambda i, j: (i, 0)),
                  pl.BlockSpec((BF16_ROWS, d), lambda i, j: (jnp.maximum(i * halo_blocks - 1, 0), 0)),
                  pl.BlockSpec((1, d), lambda i, j: (0, 0)),
                  _layer_cols(layer, d, tn, lambda i, j: j),
                  _layer_cols(layer, d, tn, lambda i, j: nj + j),
                  vec(SC_WIDTH, lambda j: j), vec(SC_WIDTH, lambda j: nj + j),
                  vec(1, lambda j: j), vec(1, lambda j: nj + j)],
        out_specs=pl.BlockSpec((tm, tn), lambda i, j: (i, j)),
        out_shape=jax.ShapeDtypeStruct((m, D_FF), BF16),
        scratch_shapes=[pltpu.VMEM((tm + BF16_ROWS, d), BF16)],
        compiler_params=_params("parallel", "arbitrary"),
        name="ffn_up",
    )(h, h, gain.reshape(1, d), w_up, w_up, conv_w, conv_w,
      conv_b.reshape(conv_b.shape[0], 1, -1), conv_b.reshape(conv_b.shape[0], 1, -1))


def _swa_body(sink_ref, q0_ref, q1_ref, kc_ref, kp_ref, vc_ref, vp_ref,
              cosc_ref, sinc_ref, cosp_ref, sinp_ref,
              cb_ref, cc_ref, ch_ref, hcc_ref, hch_ref, cw_ref, ob_ref, oc_ref):
    w = SWA_WINDOW
    n = pl.program_id(0)
    lane = lax.broadcasted_iota(jnp.int32, (w, LANES), 1)
    low_half = lane < SWA_HEAD_DIM
    first_rot = (lane & (SWA_HEAD_DIM - 1)) < SWA_HEAD_DIM // 2

    def rope(x, cos, sin):
        half = SWA_HEAD_DIM // 2
        swapped = jnp.where(first_rot, pltpu.roll(x, LANES - half, 1), pltpu.roll(x, half, 1))
        return x * cos + swapped * sin

    cosc, sinc = cosc_ref[...], sinc_ref[...]
    k_all = jnp.concatenate([rope(kp_ref[...].astype(F32), cosp_ref[...], sinp_ref[...]),
                             rope(kc_ref[...].astype(F32), cosc, sinc)], axis=0)
    v_all = jnp.concatenate([vp_ref[...], vc_ref[...]], axis=0).astype(F32)
    low2 = jnp.concatenate([low_half, low_half], axis=0)
    k_rot = pltpu.roll(k_all, SWA_HEAD_DIM, 1)
    v_rot = pltpu.roll(v_all, SWA_HEAD_DIM, 1)

    qi = lax.broadcasted_iota(jnp.int32, (w, 2 * w), 0)
    ki = lax.broadcasted_iota(jnp.int32, (w, 2 * w), 1)
    allowed = (ki > qi) & (ki <= qi + w) & ((n > 0) | (ki >= w))
    bias = jnp.where(allowed, 0.0, NEG_BIG)
    bias2 = jnp.concatenate([bias, bias], axis=0)
    row2 = lax.broadcasted_iota(jnp.int32, (2 * w, 1), 0)

    pairs = SWA_GROUP // LANES
    for g, q_ref in enumerate((q0_ref, q1_ref)):
        if g == 0:
            kg = jnp.where(low2, k_all, k_rot).astype(BF16)
            vg = jnp.where(low2, v_all, v_rot).astype(BF16)
        else:
            kg = jnp.where(low2, k_rot, k_all).astype(BF16)
            vg = jnp.where(low2, v_rot, v_all).astype(BF16)
        for p in range(pairs):
            qp = rope(q_ref[:, p * LANES:(p + 1) * LANES].astype(F32), cosc, sinc) * (SWA_HEAD_DIM ** -0.5)
            stack = jnp.concatenate([jnp.where(low_half, qp, 0.0), jnp.where(low_half, 0.0, qp)], axis=0)
            s = _dot_nt(stack.astype(BF16), kg) + bias2
            head = g * (SWA_Q_HEADS // SWA_KV_HEADS) + 2 * p
            sink = jnp.where(row2 < w, sink_ref[head], sink_ref[head + 1])
            mx = jnp.maximum(jnp.max(s, axis=-1, keepdims=True), sink)
            pr = jnp.exp(s - mx)
            denom = jnp.sum(pr, axis=-1, keepdims=True) + jnp.exp(sink - mx)
            o = _dot(pr.astype(BF16), vg) / denom
            col = (g * pairs + p) * LANES
            ob_ref[:, col:col + LANES] = jnp.where(low_half, o[:w], o[w:]).astype(ob_ref.dtype)

    u = cc_ref[...].astype(F32) * ch_ref[...].astype(F32)
    uh = hcc_ref[...].astype(F32) * hch_ref[...].astype(F32)
    uh = jnp.where(n > 0, uh, 0.0)
    ext = jnp.concatenate([uh[8:], u], axis=0)
    cw = cw_ref[...]
    y = cw[2:3] * u + cw[1:2] * ext[7:7 + w] + cw[0:1] * ext[6:6 + w]
    oc_ref[...] = (cb_ref[...].astype(F32) * y).astype(oc_ref.dtype)


def _swa_conv(z, cos_t, sin_t, sinks, conv_w):
    t = z.shape[0]
    w = SWA_WINDOW
    nb = t // w

    def cur(width, off):
        return pl.BlockSpec((w, width), lambda i: (i, off // width))

    def prev(width, off):
        return pl.BlockSpec((w, width), lambda i: (jnp.maximum(i - 1, 0), off // width))

    def halo(off):
        return pl.BlockSpec((BF16_ROWS, SC_CH),
                            lambda i: (jnp.maximum(i * (w // BF16_ROWS) - 1, 0), off // SC_CH))

    tab_c = pl.BlockSpec((w, LANES), lambda i: (i, 0))
    tab_p = pl.BlockSpec((w, LANES), lambda i: (jnp.maximum(i - 1, 0), 0))
    return pl.pallas_call(
        _swa_body,
        grid=(nb,),
        in_specs=[pl.BlockSpec(memory_space=pltpu.SMEM),
                  cur(SWA_GROUP, Z_SQ), cur(SWA_GROUP, Z_SQ + SWA_GROUP),
                  cur(SWA_KV, Z_SK), prev(SWA_KV, Z_SK), cur(SWA_KV, Z_SV), prev(SWA_KV, Z_SV),
                  tab_c, tab_c, tab_p, tab_p,
                  cur(SC_CH, Z_CB), cur(SC_CH, Z_CC), cur(SC_CH, Z_CH), halo(Z_CC), halo(Z_CH),
                  pl.BlockSpec((SC_WIDTH, SC_CH), lambda i: (0, 0))],
        out_specs=[pl.BlockSpec((w, SWA_Q), lambda i: (i, 0)),
                   pl.BlockSpec((w, SC_CH), lambda i: (i, 0))],
        out_shape=[jax.ShapeDtypeStruct((t, SWA_Q), BF16),
                   jax.ShapeDtypeStruct((t, SC_CH), BF16)],
        compiler_params=_params("parallel"),
        name="swa_conv",
    )(sinks, z, z, z, z, z, z, cos_t, sin_t, cos_t, sin_t, z, z, z, z, z, conv_w)


def _gla_constants():
    c, r = GLA_CHUNK, GLA_SUB
    t = np.arange(c)[:, None]
    s = np.arange(c)[None, :]
    blk_t, blk_s = t // r, s // r
    mats = [
        (s <= t),
        (s > t),
        (blk_s == blk_t) & (s <= t) & (s > blk_t * r),
        (blk_s == blk_t) & (s > t),
    ]
    for i in range(1, GLA_NSUB):
        mats.append((blk_t < i) & (s > blk_t * r + r - 1) & (s <= i * r))
    sel = np.concatenate(mats, axis=0).astype(np.float32)
    d_head = np.arange(GLA_QK)[:, None] // GLA_DK
    e_head = np.arange(GLA_V)[None, :] // GLA_DV
    expand = (d_head == e_head).astype(np.float32)
    return jnp.asarray(sel, BF16), jnp.asarray(expand, BF16), jnp.asarray(expand.T, F32)


def _gla_body(q_ref, k_ref, v_ref, r_ref, lr_ref, wg_ref, bg_ref, gn_ref, sel_ref, ex_ref, bd_ref,
              o_ref, st_ref, la_ref):
    c, r = GLA_CHUNK, GLA_SUB
    tc = q_ref.shape[0]

    @pl.when(pl.program_id(0) == 0)
    def _():
        st_ref[...] = jnp.zeros_like(st_ref)

    pre = _dot(lr_ref[...], wg_ref[...]) + bg_ref[...]
    la_ref[...] = (jnp.minimum(pre, 0.0) - jnp.log(1.0 + jnp.exp(-jnp.abs(pre)))) * (1.0 / GLA_GATE_TAU)

    lane_qk = lax.broadcasted_iota(jnp.int32, (r, GLA_QK), 1)
    lane_v = lax.broadcasted_iota(jnp.int32, (r, GLA_V), 1)
    row_sub = lax.broadcasted_iota(jnp.int32, (r, GLA_QK), 0)
    col_chunk = lax.broadcasted_iota(jnp.int32, (c, c), 1)

    def head_rows(x, lane, width):
        return jnp.concatenate(
            [jnp.where((lane >= h * width) & (lane < (h + 1) * width), x, 0.0) for h in range(GLA_HEADS)],
            axis=0)

    def head_diag(x, lane, width):
        out = jnp.where(lane < width, x[0:r], 0.0)
        for h in range(1, GLA_HEADS):
            out = out + jnp.where((lane >= h * width) & (lane < (h + 1) * width), x[h * r:(h + 1) * r], 0.0)
        return out

    def chunk(ci, carry):
        r0 = pl.multiple_of(ci * c, c)
        rows = pl.ds(r0, c)
        g = la_ref[rows, :]
        g_hi = g.astype(BF16)
        g_lo = (g - g_hi.astype(F32)).astype(BF16)
        cs = _dot(sel_ref[...], jnp.concatenate([g_hi, g_lo], axis=1))
        cs = cs[:, :GLA_QK] + cs[:, GLA_QK:]
        e_b = jnp.exp(cs[0:c])
        e_tail = jnp.exp(cs[c:2 * c])
        bw = cs[2 * c:3 * c]
        e_q = jnp.exp(bw)
        e_k = jnp.exp(cs[3 * c:4 * c])

        q = q_ref[rows, :].astype(F32) * (GLA_DK ** -0.5)
        k = k_ref[rows, :].astype(F32)
        v = v_ref[rows, :]
        vf = v.astype(F32)

        st = st_ref[...]
        o_inter = _dot_nt((q * e_b).astype(BF16), st.astype(BF16))
        upd = _dot_tn(v, (k * e_tail).astype(BF16))
        st_ref[...] = st * e_b[c - 1:c, :] + upd * bd_ref[...]

        qs = q * e_q
        ks = k * e_k
        atts = []
        for i in range(1, GLA_NSUB):
            ki = (ks * jnp.exp(cs[(3 + i) * c:(4 + i) * c])).astype(BF16)
            qst = head_rows(qs[i * r:(i + 1) * r], lane_qk, GLA_DK).astype(BF16)
            att = _dot_nt(qst, ki)
            atts.append(jnp.where(col_chunk < i * r, att, 0.0))
        ov = _dot(jnp.concatenate(atts, axis=0).astype(BF16), v)

        outs = []
        for i in range(GLA_NSUB):
            sub = slice(i * r, (i + 1) * r)
            qi, ki, bwi, vi = q[sub], k[sub], bw[sub], vf[sub]
            ps = []
            for s in range(r):
                e = jnp.exp(jnp.where(row_sub >= s, bwi - bwi[s:s + 1, :], NEG_BIG))
                ps.append(((qi * ki[s:s + 1, :]) * e).astype(BF16))
            rep = _dot(jnp.concatenate(ps, axis=0), ex_ref[...])
            o_i = o_inter[sub]
            for s in range(r):
                o_i = o_i + rep[s * r:(s + 1) * r] * vi[s:s + 1, :]
            if i > 0:
                o_i = o_i + head_diag(ov[(i - 1) * GLA_HEADS * r:i * GLA_HEADS * r], lane_v, GLA_DV)
            outs.append(o_i)
        o = jnp.concatenate(outs, axis=0)

        normed = []
        for h in range(GLA_HEADS):
            oh = o[:, h * GLA_DV:(h + 1) * GLA_DV]
            normed.append(oh * lax.rsqrt(jnp.mean(oh * oh, axis=-1, keepdims=True) + EPS))
        y = jnp.concatenate(normed, axis=1) * gn_ref[...]
        o_ref[rows, :] = (y * _silu(r_ref[rows, :].astype(F32))).astype(o_ref.dtype)
        return carry

    lax.fori_loop(0, tc // c, chunk, 0)


def _gla(z, w_gate, b_gate, gain, tc):
    t = z.shape[0]
    tc = min(tc, t)
    sel, expand, bd_mask = _gla_constants()

    def col(width, off):
        return pl.BlockSpec((tc, width), lambda i: (i, off // width))

    def whole(a):
        return pl.BlockSpec(a.shape, lambda i: (0, 0))

    gain_t = jnp.tile(gain.reshape(1, GLA_DV), (1, GLA_HEADS))
    b_gate = b_gate.reshape(1, GLA_QK)
    return pl.pallas_call(
        _gla_body,
        grid=(t // tc,),
        in_specs=[col(GLA_QK, Z_Q), col(GLA_QK, Z_K), col(GLA_V, Z_V), col(GLA_V, Z_R), col(LANES, Z_LR),
                  whole(w_gate), whole(b_gate), whole(gain_t), whole(sel), whole(expand), whole(bd_mask)],
        out_specs=pl.BlockSpec((tc, GLA_V), lambda i: (i, 0)),
        out_shape=jax.ShapeDtypeStruct((t, GLA_V), BF16),
        scratch_shapes=[pltpu.VMEM((GLA_V, GLA_QK), F32), pltpu.VMEM((tc, GLA_QK), F32)],
        compiler_params=_params("arbitrary"),
        name="gla",
    )(z, z, z, z, z, w_gate, b_gate, gain_t, sel, expand, bd_mask)


def _rope_tables(positions):
    inv = 1.0 / (ROPE_THETA ** (jnp.arange(0, SWA_HEAD_DIM, 2, dtype=F32) / SWA_HEAD_DIM))
    ang = positions.astype(F32)[:, None] * inv
    cos, sin = jnp.cos(ang), jnp.sin(ang)
    reps = LANES // SWA_HEAD_DIM
    return (jnp.tile(jnp.concatenate([cos, cos], axis=-1), (1, reps)),
            jnp.tile(jnp.concatenate([-sin, sin], axis=-1), (1, reps)))


def kernel(x, mem, positions, norm_mix, w_in, gla_w_gate, gla_b_gate, gla_norm, swa_sinks, sc_conv, w_out,
           norm_x, norm_mem, xa_wq, xa_wk, xa_wv, xa_wo, norm_ffn, ffn_w_up, ffn_conv, ffn_conv_b, ffn_w_down,
           norm_final):
    assert x.shape[0] == 1 and mem.shape[0] == 1
    depth = w_in.shape[0]
    h = x[0]
    m = mem[0]
    cos_t, sin_t = _rope_tables(positions[0])
    w_in_z = _reorder_w_in(w_in, 256)
    w_gate = jnp.pad(gla_w_gate, ((0, 0), (0, LANES - GLA_GATE_RANK), (0, 0))).astype(BF16)

    for l in range(depth):
        z = _norm_matmul(h, norm_mix[l], w_in_z, l, 1024, 640, "in_proj")
        o_a = _gla(z, w_gate[l], gla_b_gate[l], gla_norm[l], 256)
        o_b, o_c = _swa_conv(z, cos_t, sin_t, swa_sinks[l], sc_conv[l])
        h = _matmul_resid([o_a, o_b, o_c], w_out, l, h, 512, 1024, "out_proj")
        mem_k = _norm_matmul(m, norm_mem[l], xa_wk, l, N_MEM, 512, "mem_k")
        mem_v = _norm_matmul(m, norm_mem[l], xa_wv, l, N_MEM, 512, "mem_v")
        o_x = _cross_attention(h, norm_x[l], xa_wq, l, mem_k, mem_v, 1024)
        h = _matmul_resid([o_x], xa_wo, l, h, 512, 1024, "xa_out_proj")
        act = _ffn_up(h, norm_ffn[l], ffn_w_up, ffn_conv, ffn_conv_b, l, 1024, 512)
        h = _matmul_resid([act], ffn_w_down, l, h, 512, 512, "ffn_down")
    return _rmsnorm(h, norm_final, F32, 512)[None]
```

```python
import functools

import numpy as np
import jax
import jax.numpy as jnp
from jax import lax
from jax.experimental import pallas as pl
from jax.experimental.pallas import tpu as pltpu

F32 = jnp.float32
BF16 = jnp.bfloat16

D_MODEL = 2048
N_MEM = 256
XA_HEADS = 4
XA_HEAD_DIM = D_MODEL // XA_HEADS
GLA_HEADS = 4
GLA_DK = 64
GLA_DV = 128
GLA_GATE_RANK = 16
GLA_GATE_TAU = 16.0
SWA_Q_HEADS = 16
SWA_KV_HEADS = 2
SWA_HEAD_DIM = 64
SWA_WINDOW = 128
ROPE_THETA = 10000.0
SC_CH = 512
SC_WIDTH = 3
D_FF = 5632
EPS = 1e-6

GLA_QK = GLA_HEADS * GLA_DK
GLA_V = GLA_HEADS * GLA_DV
SWA_Q = SWA_Q_HEADS * SWA_HEAD_DIM
SWA_KV = SWA_KV_HEADS * SWA_HEAD_DIM
SWA_GROUP = SWA_Q // SWA_KV_HEADS
IN_SIZES = (GLA_QK, GLA_QK, GLA_V, GLA_V, GLA_GATE_RANK, SWA_Q, SWA_KV, SWA_KV, SC_CH, SC_CH, SC_CH)
N_IN = sum(IN_SIZES)

LANES = 128
BF16_ROWS = 16
VMEM_LIMIT = 56 * 1024 * 1024

Z_Q, Z_K, Z_V, Z_R = 0, 256, 512, 1024
Z_SQ, Z_CB, Z_CC, Z_CH = 1536, 2560, 3072, 3584
Z_SK, Z_SV, Z_LR = 4096, 4224, 4352
Z_COLS = 4480

GLA_CHUNK = 64
GLA_SUB = 16
GLA_NSUB = GLA_CHUNK // GLA_SUB
NEG_BIG = -1e30


def _params(*sem):
    return pltpu.CompilerParams(dimension_semantics=sem, vmem_limit_bytes=VMEM_LIMIT)


def _dot(a, b):
    return jnp.dot(a, b, preferred_element_type=F32)


def _dot_nt(a, b):
    return lax.dot_general(a, b, (((1,), (1,)), ((), ())), preferred_element_type=F32)


def _dot_tn(a, b):
    return lax.dot_general(a, b, (((0,), (0,)), ((), ())), preferred_element_type=F32)


def _silu(x):
    return x / (1.0 + jnp.exp(-x))


def _normalize(x, gain):
    return x * lax.rsqrt(jnp.mean(x * x, axis=-1, keepdims=True) + EPS) * gain


def _layer_cols(layer, k, tn, col_of):
    return pl.BlockSpec((None, k, tn), lambda *ids: (layer, 0, col_of(*ids)))


def _rmsnorm_body(x_ref, g_ref, o_ref):
    o_ref[...] = _normalize(x_ref[...], g_ref[...]).astype(o_ref.dtype)


def _rmsnorm(x, gain, out_dtype, tm):
    m, d = x.shape
    tm = min(tm, m)
    return pl.pallas_call(
        _rmsnorm_body,
        grid=(m // tm,),
        in_specs=[pl.BlockSpec((tm, d), lambda i: (i, 0)),
                  pl.BlockSpec((1, d), lambda i: (0, 0))],
        out_specs=pl.BlockSpec((tm, d), lambda i: (i, 0)),
        out_shape=jax.ShapeDtypeStruct((m, d), out_dtype),
        compiler_params=_params("parallel"),
        name="rmsnorm",
    )(x, gain.reshape(1, d))


def _reorder_body(wt_ref, o_ref):
    o = np.cumsum((0,) + IN_SIZES)
    wt = wt_ref[...]
    pad = jnp.zeros((LANES - GLA_GATE_RANK, wt.shape[1]), wt.dtype)
    zt = jnp.concatenate(
        [wt[o[0]:o[4]], wt[o[5]:o[6]], wt[o[8]:o[11]], wt[o[6]:o[8]], wt[o[4]:o[5]], pad], axis=0)
    o_ref[...] = zt.T.astype(o_ref.dtype)


def _reorder_w_in(w_in_t, tk):
    depth, n, d = w_in_t.shape
    return pl.pallas_call(
        _reorder_body,
        grid=(depth, d // tk),
        in_specs=[pl.BlockSpec((None, n, tk), lambda l, i: (l, 0, i))],
        out_specs=pl.BlockSpec((None, tk, Z_COLS), lambda l, i: (l, i, 0)),
        out_shape=jax.ShapeDtypeStruct((depth, d, Z_COLS), BF16),
        compiler_params=_params("parallel", "parallel"),
        name="reorder_w_in",
    )(w_in_t)


def _cast_body(w_ref, o_ref):
    o_ref[...] = w_ref[...].astype(o_ref.dtype)


def _cast_bf16(w, tk):
    depth, k, n = w.shape
    spec = pl.BlockSpec((None, tk, n), lambda l, i: (l, i, 0))
    return pl.pallas_call(
        _cast_body,
        grid=(depth, k // tk),
        in_specs=[spec],
        out_specs=spec,
        out_shape=jax.ShapeDtypeStruct(w.shape, BF16),
        compiler_params=_params("parallel", "parallel"),
        name="cast_bf16",
    )(w)


def _norm_matmul_body(h_ref, g_ref, w_ref, o_ref, hn_ref):
    @pl.when(pl.program_id(1) == 0)
    def _():
        hn_ref[...] = _normalize(h_ref[...], g_ref[...]).astype(BF16)

    o_ref[...] = _dot(hn_ref[...], w_ref[...].astype(BF16)).astype(o_ref.dtype)


def _norm_matmul(h, gain, w, layer, tm, tn, name):
    m, d = h.shape
    n = w.shape[2]
    tm = min(tm, m)
    return pl.pallas_call(
        _norm_matmul_body,
        grid=(m // tm, n // tn),
        in_specs=[pl.BlockSpec((tm, d), lambda i, j: (i, 0)),
                  pl.BlockSpec((1, d), lambda i, j: (0, 0)),
                  _layer_cols(layer, d, tn, lambda i, j: j)],
        out_specs=pl.BlockSpec((tm, tn), lambda i, j: (i, j)),
        out_shape=jax.ShapeDtypeStruct((m, n), BF16),
        scratch_shapes=[pltpu.VMEM((tm, d), BF16)],
        compiler_params=_params("parallel", "arbitrary"),
        name=name,
    )(h, gain.reshape(1, d), w)


def _resid_body(*refs, npieces):
    a_refs = refs[:npieces]
    w_ref, h_ref, o_ref, wb_ref = refs[npieces:]

    @pl.when(pl.program_id(1) == 0)
    def _():
        wb_ref[...] = w_ref[...].astype(BF16)

    if npieces == 1:
        a = a_refs[0][...]
    else:
        a = jnp.concatenate([r[...] for r in a_refs], axis=1)
    o_ref[...] = h_ref[...] + _dot(a, wb_ref[...])


def _matmul_resid(pieces, w, layer, h, tm, tn, name):
    m, d = h.shape
    k = w.shape[1]
    tm = min(tm, m)
    assert sum(p.shape[1] for p in pieces) == k
    in_specs = [pl.BlockSpec((tm, p.shape[1]), lambda j, i: (i, 0)) for p in pieces]
    in_specs += [_layer_cols(layer, k, tn, lambda j, i: j),
                 pl.BlockSpec((tm, tn), lambda j, i: (i, j))]
    return pl.pallas_call(
        functools.partial(_resid_body, npieces=len(pieces)),
        grid=(d // tn, m // tm),
        in_specs=in_specs,
        out_specs=pl.BlockSpec((tm, tn), lambda j, i: (i, j)),
        out_shape=jax.ShapeDtypeStruct((m, d), F32),
        scratch_shapes=[pltpu.VMEM((k, tn), BF16)],
        compiler_params=_params("parallel", "arbitrary"),
        name=name,
    )(*pieces, w, h)


def _xa_body(h_ref, g_ref, wq_ref, k_ref, v_ref, o_ref, hn_ref):
    @pl.when(pl.program_id(1) == 0)
    def _():
        hn_ref[...] = _normalize(h_ref[...], g_ref[...]).astype(BF16)

    q = _dot(hn_ref[...], wq_ref[...].astype(BF16)).astype(BF16)
    s = _dot_nt(q, k_ref[...]) * (XA_HEAD_DIM ** -0.5)
    m = jnp.max(s, axis=-1, keepdims=True)
    p = jnp.exp(s - m)
    denom = jnp.sum(p, axis=-1, keepdims=True)
    o = _dot(p.astype(BF16), v_ref[...]) / denom
    o_ref[...] = o.astype(o_ref.dtype)


def _cross_attention(h, gain, wq, layer, mem_k, mem_v, tm):
    m, d = h.shape
    tm = min(tm, m)
    hd = XA_HEAD_DIM
    return pl.pallas_call(
        _xa_body,
        grid=(m // tm, XA_HEADS),
        in_specs=[pl.BlockSpec((tm, d), lambda i, j: (i, 0)),
                  pl.BlockSpec((1, d), lambda i, j: (0, 0)),
                  _layer_cols(layer, d, hd, lambda i, j: j),
                  pl.BlockSpec((N_MEM, hd), lambda i, j: (0, j)),
                  pl.BlockSpec((N_MEM, hd), lambda i, j: (0, j))],
        out_specs=pl.BlockSpec((tm, hd), lambda i, j: (i, j)),
        out_shape=jax.ShapeDtypeStruct((m, d), BF16),
        scratch_shapes=[pltpu.VMEM((tm, d), BF16)],
        compiler_params=_params("parallel", "arbitrary"),
        name="cross_attention",
    )(h, gain.reshape(1, d), wq, mem_k, mem_v)


def _ffn_up_body(h_ref, halo_ref, g_ref, wg_ref, wv_ref, cg_ref, cv_ref, bg_ref, bv_ref, o_ref, hn_ref):
    tm = h_ref.shape[0]

    @pl.when(pl.program_id(1) == 0)
    def _():
        gain = g_ref[...]
        halo = jnp.where(pl.program_id(0) > 0, _normalize(halo_ref[...], gain), 0.0)
        hn_ref[0:BF16_ROWS, :] = halo.astype(BF16)
        hn_ref[BF16_ROWS:, :] = _normalize(h_ref[...], gain).astype(BF16)

    a = hn_ref[BF16_ROWS:, :]
    halo = hn_ref[0:BF16_ROWS, :]

    def conv_branch(w_ref, c_ref, b_ref):
        w = w_ref[...].astype(BF16)
        u = _dot(a, w)
        uh = _dot(halo, w)
        ext = jnp.concatenate([uh[8:], u], axis=0)
        c = c_ref[...]
        return (c[2:3] * u + c[1:2] * ext[7:7 + tm] + c[0:1] * ext[6:6 + tm]) + b_ref[...]

    g = conv_branch(wg_ref, cg_ref, bg_ref)
    val = conv_branch(wv_ref, cv_ref, bv_ref)
    o_ref[...] = (_silu(g) * val).astype(o_ref.dtype)


def _ffn_up(h, gain, w_up, conv_w, conv_b, layer, tm, tn):
    m, d = h.shape
    tm = min(tm, m)
    nj = D_FF // tn
    halo_blocks = tm // BF16_ROWS

    def vec(rows, col_of):
        return pl.BlockSpec((None, rows, tn), lambda i, j: (layer, 0, col_of(j)))

    return pl.pallas_call(
        _ffn_up_body,
        grid=(m // tm, nj),
        in_specs=[pl.BlockSpec((tm, d), lambda i, j: (i, 0)),
                  pl.BlockSpec((BF16_ROWS, d), lambda i, j: (jnp.maximum(i * halo_blocks - 1, 0), 0)),
                  pl.BlockSpec((1, d), lambda i, j: (0, 0)),
                  _layer_cols(layer, d, tn, lambda i, j: j),
                  _layer_cols(layer, d, tn, lambda i, j: nj + j),
                  vec(SC_WIDTH, lambda j: j), vec(SC_WIDTH, lambda j: nj + j),
                  vec(1, lambda j: j), vec(1, lambda j: nj + j)],
        out_specs=pl.BlockSpec((tm, tn), lambda i, j: (i, j)),
        out_shape=jax.ShapeDtypeStruct((m, D_FF), BF16),
        scratch_shapes=[pltpu.VMEM((tm + BF16_ROWS, d), BF16)],
        compiler_params=_params("parallel", "arbitrary"),
        name="ffn_up",
    )(h, h, gain.reshape(1, d), w_up, w_up, conv_w, conv_w,
      conv_b.reshape(conv_b.shape[0], 1, -1), conv_b.reshape(conv_b.shape[0], 1, -1))


SWA_STEP = 2 * SWA_WINDOW


def _swa_out_body(sink_ref, q0_ref, q1_ref, kc_ref, kp_ref, vc_ref, vp_ref,
                  cosc_ref, sinc_ref, cosp_ref, sinp_ref,
                  cb_ref, cc_ref, ch_ref, hcc_ref, hch_ref, cw_ref,
                  oa_ref, h_ref, wo_ref, o_ref, mix_ref, *, last_tile):
    w = SWA_WINDOW

    @pl.when(pl.program_id(0) == 0)
    def _():
        mix_ref[...] = jnp.zeros_like(mix_ref)

    a = jnp.concatenate([oa_ref[...], mix_ref[...]], axis=1)
    o_ref[...] = h_ref[...] + _dot(a, wo_ref[...])

    n = jnp.minimum(pl.program_id(0), last_tile)
    lane = lax.broadcasted_iota(jnp.int32, (1, LANES), 1)
    low_half = lane < SWA_HEAD_DIM
    first_rot = (lane & (SWA_HEAD_DIM - 1)) < SWA_HEAD_DIM // 2

    def rope(x, cos, sin):
        half = SWA_HEAD_DIM // 2
        swapped = jnp.where(first_rot, pltpu.roll(x, LANES - half, 1), pltpu.roll(x, half, 1))
        return x * cos + swapped * sin

    cosc, sinc = cosc_ref[...], sinc_ref[...]
    k_cur = rope(kc_ref[...].astype(F32), cosc, sinc)
    k_prev = rope(kp_ref[...].astype(F32), cosp_ref[...], sinp_ref[...])
    v_cur = vc_ref[...].astype(F32)
    v_prev = vp_ref[...].astype(F32)

    qi = lax.broadcasted_iota(jnp.int32, (w, 2 * w), 0)
    ki = lax.broadcasted_iota(jnp.int32, (w, 2 * w), 1)
    in_window = (ki > qi) & (ki <= qi + w)
    row2 = lax.broadcasted_iota(jnp.int32, (2 * w, 1), 0)

    pairs = SWA_GROUP // LANES
    for b in range(SWA_STEP // w):
        rows = slice(b * w, (b + 1) * w)
        if b == 0:
            k_all = jnp.concatenate([k_prev, k_cur[rows]], axis=0)
            v_all = jnp.concatenate([v_prev, v_cur[rows]], axis=0)
            allowed = in_window & ((n > 0) | (ki >= w))
        else:
            k_all = k_cur[(b - 1) * w:(b + 1) * w]
            v_all = v_cur[(b - 1) * w:(b + 1) * w]
            allowed = in_window
        bias = jnp.where(allowed, 0.0, NEG_BIG)
        bias2 = jnp.concatenate([bias, bias], axis=0)
        k_rot = pltpu.roll(k_all, SWA_HEAD_DIM, 1)
        v_rot = pltpu.roll(v_all, SWA_HEAD_DIM, 1)
        for g, q_ref in enumerate((q0_ref, q1_ref)):
            if g == 0:
                kg = jnp.where(low_half, k_all, k_rot).astype(BF16)
                vg = jnp.where(low_half, v_all, v_rot).astype(BF16)
            else:
                kg = jnp.where(low_half, k_rot, k_all).astype(BF16)
                vg = jnp.where(low_half, v_rot, v_all).astype(BF16)
            for p in range(pairs):
                qp = rope(q_ref[rows, p * LANES:(p + 1) * LANES].astype(F32), cosc[rows], sinc[rows])
                qp = qp * (SWA_HEAD_DIM ** -0.5)
                stack = jnp.concatenate([jnp.where(low_half, qp, 0.0), jnp.where(low_half, 0.0, qp)], axis=0)
                s = _dot_nt(stack.astype(BF16), kg) + bias2
                head = g * (SWA_Q_HEADS // SWA_KV_HEADS) + 2 * p
                sink = jnp.where(row2 < w, sink_ref[head], sink_ref[head + 1])
                mx = jnp.maximum(jnp.max(s, axis=-1, keepdims=True), sink)
                pr = jnp.exp(s - mx)
                denom = jnp.sum(pr, axis=-1, keepdims=True) + jnp.exp(sink - mx)
                o = _dot(pr.astype(BF16), vg) / denom
                col = (g * pairs + p) * LANES
                mix_ref[rows, col:col + LANES] = jnp.where(low_half, o[:w], o[w:]).astype(mix_ref.dtype)

    u = cc_ref[...].astype(F32) * ch_ref[...].astype(F32)
    uh = hcc_ref[...].astype(F32) * hch_ref[...].astype(F32)
    uh = jnp.where(n > 0, uh, 0.0)
    ext = jnp.concatenate([uh[8:], u], axis=0)
    cw = cw_ref[...]
    y = cw[2:3] * u + cw[1:2] * ext[7:7 + SWA_STEP] + cw[0:1] * ext[6:6 + SWA_STEP]
    mix_ref[:, SWA_Q:] = (cb_ref[...].astype(F32) * y).astype(mix_ref.dtype)


def _swa_conv_out(z, o_a, h, w_out, layer, cos_t, sin_t, sinks, conv_w):
    t, d = h.shape
    w, st = SWA_WINDOW, SWA_STEP
    nt = t // st

    def tile(i):
        return jnp.minimum(i, nt - 1)

    def cur(width, off):
        return pl.BlockSpec((st, width), lambda i: (tile(i), off // width))

    def prev(width, off):
        return pl.BlockSpec((w, width), lambda i: (jnp.maximum(tile(i) * (st // w) - 1, 0), off // width))

    def halo(off):
        return pl.BlockSpec((BF16_ROWS, SC_CH),
                            lambda i: (jnp.maximum(tile(i) * (st // BF16_ROWS) - 1, 0), off // SC_CH))

    def lagged(width):
        return pl.BlockSpec((st, width), lambda i: (jnp.maximum(i - 1, 0), 0))

    tab_c = pl.BlockSpec((st, LANES), lambda i: (tile(i), 0))
    tab_p = pl.BlockSpec((w, LANES), lambda i: (jnp.maximum(tile(i) * (st // w) - 1, 0), 0))
    return pl.pallas_call(
        functools.partial(_swa_out_body, last_tile=nt - 1),
        grid=(nt + 1,),
        in_specs=[pl.BlockSpec(memory_space=pltpu.SMEM),
                  cur(SWA_GROUP, Z_SQ), cur(SWA_GROUP, Z_SQ + SWA_GROUP),
                  cur(SWA_KV, Z_SK), prev(SWA_KV, Z_SK), cur(SWA_KV, Z_SV), prev(SWA_KV, Z_SV),
                  tab_c, tab_c, tab_p, tab_p,
                  cur(SC_CH, Z_CB), cur(SC_CH, Z_CC), cur(SC_CH, Z_CH), halo(Z_CC), halo(Z_CH),
                  pl.BlockSpec((SC_WIDTH, SC_CH), lambda i: (0, 0)),
                  lagged(GLA_V), lagged(d),
                  pl.BlockSpec((None, d, d), lambda i: (layer, 0, 0))],
        out_specs=lagged(d),
        out_shape=jax.ShapeDtypeStruct((t, d), F32),
        scratch_shapes=[pltpu.VMEM((st, SWA_Q + SC_CH), BF16)],
        compiler_params=_params("arbitrary"),
        name="swa_conv_out",
    )(sinks, z, z, z, z, z, z, cos_t, sin_t, cos_t, sin_t, z, z, z, z, z, conv_w, o_a, h, w_out)


def _gla_constants():
    c, r = GLA_CHUNK, GLA_SUB
    t = np.arange(c)[:, None]
    s = np.arange(c)[None, :]
    blk_t, blk_s = t // r, s // r
    mats = [
        (s <= t),
        (s > t),
        (blk_s == blk_t) & (s <= t) & (s > blk_t * r),
        (blk_s == blk_t) & (s > t),
    ]
    for i in range(1, GLA_NSUB):
        mats.append((blk_t < i) & (s > blk_t * r + r - 1) & (s <= i * r))
    sel = np.concatenate(mats, axis=0).astype(np.float32)
    d_head = np.arange(GLA_QK)[:, None] // GLA_DK
    e_head = np.arange(GLA_V)[None, :] // GLA_DV
    expand = (d_head == e_head).astype(np.float32)
    return jnp.asarray(sel, BF16), jnp.asarray(expand, BF16), jnp.asarray(expand.T, F32)


def _gla_body(q_ref, k_ref, v_ref, r_ref, lr_ref, wg_ref, bg_ref, gn_ref, sel_ref, ex_ref, bd_ref,
              o_ref, st_ref, la_ref):
    c, r = GLA_CHUNK, GLA_SUB
    tc = q_ref.shape[0]

    @pl.when(pl.program_id(0) == 0)
    def _():
        st_ref[...] = jnp.zeros_like(st_ref)

    pre = _dot(lr_ref[...], wg_ref[...]) + bg_ref[...]
    la_ref[...] = (jnp.minimum(pre, 0.0) - jnp.log(1.0 + jnp.exp(-jnp.abs(pre)))) * (1.0 / GLA_GATE_TAU)

    lane_qk = lax.broadcasted_iota(jnp.int32, (r, GLA_QK), 1)
    lane_v = lax.broadcasted_iota(jnp.int32, (r, GLA_V), 1)
    row_sub = lax.broadcasted_iota(jnp.int32, (r, GLA_QK), 0)
    col_chunk = lax.broadcasted_iota(jnp.int32, (c, c), 1)

    def head_rows(x, lane, width):
        return jnp.concatenate(
            [jnp.where((lane >= h * width) & (lane < (h + 1) * width), x, 0.0) for h in range(GLA_HEADS)],
            axis=0)

    def head_diag(x, lane, width):
        out = jnp.where(lane < width, x[0:r], 0.0)
        for h in range(1, GLA_HEADS):
            out = out + jnp.where((lane >= h * width) & (lane < (h + 1) * width), x[h * r:(h + 1) * r], 0.0)
        return out

    def chunk(ci, carry):
        r0 = pl.multiple_of(ci * c, c)
        rows = pl.ds(r0, c)
        g = la_ref[rows, :]
        g_hi = g.astype(BF16)
        g_lo = (g - g_hi.astype(F32)).astype(BF16)
        cs = _dot(sel_ref[...], jnp.concatenate([g_hi, g_lo], axis=1))
        cs = cs[:, :GLA_QK] + cs[:, GLA_QK:]
        e_b = jnp.exp(cs[0:c])
        e_tail = jnp.exp(cs[c:2 * c])
        bw = cs[2 * c:3 * c]
        e_q = jnp.exp(bw)
        e_k = jnp.exp(cs[3 * c:4 * c])

        q = q_ref[rows, :].astype(F32) * (GLA_DK ** -0.5)
        k = k_ref[rows, :].astype(F32)
        v = v_ref[rows, :]
        vf = v.astype(F32)

        st = st_ref[...]
        o_inter = _dot_nt((q * e_b).astype(BF16), st.astype(BF16))
        upd = _dot_tn(v, (k * e_tail).astype(BF16))
        st_ref[...] = st * e_b[c - 1:c, :] + upd * bd_ref[...]

        qs = q * e_q
        ks = k * e_k
        atts = []
        for i in range(1, GLA_NSUB):
            ki = (ks * jnp.exp(cs[(3 + i) * c:(4 + i) * c])).astype(BF16)
            qst = head_rows(qs[i * r:(i + 1) * r], lane_qk, GLA_DK).astype(BF16)
            att = _dot_nt(qst, ki)
            atts.append(jnp.where(col_chunk < i * r, att, 0.0))
        ov = _dot(jnp.concatenate(atts, axis=0).astype(BF16), v)

        outs = []
        for i in range(GLA_NSUB):
            sub = slice(i * r, (i + 1) * r)
            qi, ki, bwi, vi = q[sub], k[sub], bw[sub], vf[sub]
            ps = []
            for s in range(r):
                e = jnp.exp(jnp.where(row_sub >= s, bwi - bwi[s:s + 1, :], NEG_BIG))
                ps.append(((qi * ki[s:s + 1, :]) * e).astype(BF16))
            rep = _dot(jnp.concatenate(ps, axis=0), ex_ref[...])
            o_i = o_inter[sub]
            for s in range(r):
                o_i = o_i + rep[s * r:(s + 1) * r] * vi[s:s + 1, :]
            if i > 0:
                o_i = o_i + head_diag(ov[(i - 1) * GLA_HEADS * r:i * GLA_HEADS * r], lane_v, GLA_DV)
            outs.append(o_i)
        o = jnp.concatenate(outs, axis=0)

        normed = []
        for h in range(GLA_HEADS):
            oh = o[:, h * GLA_DV:(h + 1) * GLA_DV]
            normed.append(oh * lax.rsqrt(jnp.mean(oh * oh, axis=-1, keepdims=True) + EPS))
        y = jnp.concatenate(normed, axis=1) * gn_ref[...]
        o_ref[rows, :] = (y * _silu(r_ref[rows, :].astype(F32))).astype(o_ref.dtype)
        return carry

    lax.fori_loop(0, tc // c, chunk, 0)


def _gla(z, w_gate, b_gate, gain, tc):
    t = z.shape[0]
    tc = min(tc, t)
    sel, expand, bd_mask = _gla_constants()

    def col(width, off):
        return pl.BlockSpec((tc, width), lambda i: (i, off // width))

    def whole(a):
        return pl.BlockSpec(a.shape, lambda i: (0, 0))

    gain_t = jnp.tile(gain.reshape(1, GLA_DV), (1, GLA_HEADS))
    b_gate = b_gate.reshape(1, GLA_QK)
    return pl.pallas_call(
        _gla_body,
        grid=(t // tc,),
        in_specs=[col(GLA_QK, Z_Q), col(GLA_QK, Z_K), col(GLA_V, Z_V), col(GLA_V, Z_R), col(LANES, Z_LR),
                  whole(w_gate), whole(b_gate), whole(gain_t), whole(sel), whole(expand), whole(bd_mask)],
        out_specs=pl.BlockSpec((tc, GLA_V), lambda i: (i, 0)),
        out_shape=jax.ShapeDtypeStruct((t, GLA_V), BF16),
        scratch_shapes=[pltpu.VMEM((GLA_V, GLA_QK), F32), pltpu.VMEM((tc, GLA_QK), F32)],
        compiler_params=_params("arbitrary"),
        name="gla",
    )(z, z, z, z, z, w_gate, b_gate, gain_t, sel, expand, bd_mask)


def _rope_tables(positions):
    inv = 1.0 / (ROPE_THETA ** (jnp.arange(0, SWA_HEAD_DIM, 2, dtype=F32) / SWA_HEAD_DIM))
    ang = positions.astype(F32)[:, None] * inv
    cos, sin = jnp.cos(ang), jnp.sin(ang)
    reps = LANES // SWA_HEAD_DIM
    return (jnp.tile(jnp.concatenate([cos, cos], axis=-1), (1, reps)),
            jnp.tile(jnp.concatenate([-sin, sin], axis=-1), (1, reps)))


def kernel(x, mem, positions, norm_mix, w_in, gla_w_gate, gla_b_gate, gla_norm, swa_sinks, sc_conv, w_out,
           norm_x, norm_mem, xa_wq, xa_wk, xa_wv, xa_wo, norm_ffn, ffn_w_up, ffn_conv, ffn_conv_b, ffn_w_down,
           norm_final):
    assert x.shape[0] == 1 and mem.shape[0] == 1
    depth = w_in.shape[0]
    h = x[0]
    m = mem[0]
    cos_t, sin_t = _rope_tables(positions[0])
    w_in_z = _reorder_w_in(jnp.swapaxes(w_in, 1, 2), 256)
    w_out_b = _cast_bf16(w_out, 512)
    w_gate = jnp.pad(gla_w_gate, ((0, 0), (0, LANES - GLA_GATE_RANK), (0, 0))).astype(BF16)

    for l in range(depth):
        z = _norm_matmul(h, norm_mix[l], w_in_z, l, 1024, 640, "in_proj")
        o_a = _gla(z, w_gate[l], gla_b_gate[l], gla_norm[l], 256)
        h = _swa_conv_out(z, o_a, h, w_out_b, l, cos_t, sin_t, swa_sinks[l], sc_conv[l])
        mem_k = _norm_matmul(m, norm_mem[l], xa_wk, l, N_MEM, 512, "mem_k")
        mem_v = _norm_matmul(m, norm_mem[l], xa_wv, l, N_MEM, 512, "mem_v")
        o_x = _cross_attention(h, norm_x[l], xa_wq, l, mem_k, mem_v, 1024)
        h = _matmul_resid([o_x], xa_wo, l, h, 512, 1024, "xa_out_proj")
        act = _ffn_up(h, norm_ffn[l], ffn_w_up, ffn_conv, ffn_conv_b, l, 1024, 512)
        h = _matmul_resid([act], ffn_w_down, l, h, 512, 512, "ffn_down")
    return _rmsnorm(h, norm_final, F32, 512)[None]
```

```python
import functools

import numpy as np
import jax
import jax.numpy as jnp
from jax import lax
from jax.experimental import pallas as pl
from jax.experimental.pallas import tpu as pltpu

F32 = jnp.float32
BF16 = jnp.bfloat16

D_MODEL = 2048
N_MEM = 256
XA_HEADS = 4
XA_HEAD_DIM = D_MODEL // XA_HEADS
GLA_HEADS = 4
GLA_DK = 64
GLA_DV = 128
GLA_GATE_RANK = 16
GLA_GATE_TAU = 16.0
SWA_Q_HEADS = 16
SWA_KV_HEADS = 2
SWA_HEAD_DIM = 64
SWA_WINDOW = 128
ROPE_THETA = 10000.0
SC_CH = 512
SC_WIDTH = 3
D_FF = 5632
EPS = 1e-6

GLA_QK = GLA_HEADS * GLA_DK
GLA_V = GLA_HEADS * GLA_DV
SWA_Q = SWA_Q_HEADS * SWA_HEAD_DIM
SWA_KV = SWA_KV_HEADS * SWA_HEAD_DIM
SWA_GROUP = SWA_Q // SWA_KV_HEADS
IN_SIZES = (GLA_QK, GLA_QK, GLA_V, GLA_V, GLA_GATE_RANK, SWA_Q, SWA_KV, SWA_KV, SC_CH, SC_CH, SC_CH)
N_IN = sum(IN_SIZES)

LANES = 128
BF16_ROWS = 16
VMEM_LIMIT = 56 * 1024 * 1024

Z_Q, Z_K, Z_V, Z_R = 0, 256, 512, 1024
Z_SQ, Z_CB, Z_CC, Z_CH = 1536, 2560, 3072, 3584
Z_SK, Z_SV, Z_LR = 4096, 4224, 4352
Z_COLS = 4480

GLA_CHUNK = 64
GLA_SUB = 16
GLA_NSUB = GLA_CHUNK // GLA_SUB
NEG_BIG = -1e30


def _params(*sem):
    return pltpu.CompilerParams(dimension_semantics=sem, vmem_limit_bytes=VMEM_LIMIT)


def _dot(a, b):
    return jnp.dot(a, b, preferred_element_type=F32)


def _dot_nt(a, b):
    return lax.dot_general(a, b, (((1,), (1,)), ((), ())), preferred_element_type=F32)


def _dot_tn(a, b):
    return lax.dot_general(a, b, (((0,), (0,)), ((), ())), preferred_element_type=F32)


def _silu(x):
    return x / (1.0 + jnp.exp(-x))


def _normalize(x, gain):
    return x * lax.rsqrt(jnp.mean(x * x, axis=-1, keepdims=True) + EPS) * gain


def _layer_cols(layer, k, tn, col_of):
    return pl.BlockSpec((None, k, tn), lambda *ids: (layer, 0, col_of(*ids)))


def _rmsnorm_body(x_ref, g_ref, o_ref):
    o_ref[...] = _normalize(x_ref[...], g_ref[...]).astype(o_ref.dtype)


def _rmsnorm(x, gain, out_dtype, tm):
    m, d = x.shape
    tm = min(tm, m)
    return pl.pallas_call(
        _rmsnorm_body,
        grid=(m // tm,),
        in_specs=[pl.BlockSpec((tm, d), lambda i: (i, 0)),
                  pl.BlockSpec((1, d), lambda i: (0, 0))],
        out_specs=pl.BlockSpec((tm, d), lambda i: (i, 0)),
        out_shape=jax.ShapeDtypeStruct((m, d), out_dtype),
        compiler_params=_params("parallel"),
        name="rmsnorm",
    )(x, gain.reshape(1, d))


def _reorder_body(wt_ref, o_ref):
    o = np.cumsum((0,) + IN_SIZES)
    wt = wt_ref[...]
    pad = jnp.zeros((LANES - GLA_GATE_RANK, wt.shape[1]), wt.dtype)
    zt = jnp.concatenate(
        [wt[o[0]:o[4]], wt[o[5]:o[6]], wt[o[8]:o[11]], wt[o[6]:o[8]], wt[o[4]:o[5]], pad], axis=0)
    o_ref[...] = zt.T.astype(o_ref.dtype)


def _reorder_w_in(w_in_t, tk):
    depth, n, d = w_in_t.shape
    return pl.pallas_call(
        _reorder_body,
        grid=(depth, d // tk),
        in_specs=[pl.BlockSpec((None, n, tk), lambda l, i: (l, 0, i))],
        out_specs=pl.BlockSpec((None, tk, Z_COLS), lambda l, i: (l, i, 0)),
        out_shape=jax.ShapeDtypeStruct((depth, d, Z_COLS), BF16),
        compiler_params=_params("parallel", "parallel"),
        name="reorder_w_in",
    )(w_in_t)


def _cast_body(w_ref, o_ref):
    o_ref[...] = w_ref[...].astype(o_ref.dtype)


def _cast_bf16(w, tk):
    depth, k, n = w.shape
    spec = pl.BlockSpec((None, tk, n), lambda l, i: (l, i, 0))
    return pl.pallas_call(
        _cast_body,
        grid=(depth, k // tk),
        in_specs=[spec],
        out_specs=spec,
        out_shape=jax.ShapeDtypeStruct(w.shape, BF16),
        compiler_params=_params("parallel", "parallel"),
        name="cast_bf16",
    )(w)


def _norm_matmul_body(h_ref, g_ref, w_ref, o_ref, hn_ref):
    @pl.when(pl.program_id(1) == 0)
    def _():
        hn_ref[...] = _normalize(h_ref[...], g_ref[...]).astype(BF16)

    o_ref[...] = _dot(hn_ref[...], w_ref[...].astype(BF16)).astype(o_ref.dtype)


def _norm_matmul(h, gain, w, layer, tm, tn, name):
    m, d = h.shape
    n = w.shape[2]
    tm = min(tm, m)
    return pl.pallas_call(
        _norm_matmul_body,
        grid=(m // tm, n // tn),
        in_specs=[pl.BlockSpec((tm, d), lambda i, j: (i, 0)),
                  pl.BlockSpec((1, d), lambda i, j: (0, 0)),
                  _layer_cols(layer, d, tn, lambda i, j: j)],
        out_specs=pl.BlockSpec((tm, tn), lambda i, j: (i, j)),
        out_shape=jax.ShapeDtypeStruct((m, n), BF16),
        scratch_shapes=[pltpu.VMEM((tm, d), BF16)],
        compiler_params=_params("parallel", "arbitrary"),
        name=name,
    )(h, gain.reshape(1, d), w)


def _resid_body(*refs, npieces):
    a_refs = refs[:npieces]
    w_ref, h_ref, o_ref, wb_ref = refs[npieces:]

    @pl.when(pl.program_id(1) == 0)
    def _():
        wb_ref[...] = w_ref[...].astype(BF16)

    if npieces == 1:
        a = a_refs[0][...]
    else:
        a = jnp.concatenate([r[...] for r in a_refs], axis=1)
    o_ref[...] = h_ref[...] + _dot(a, wb_ref[...])


def _matmul_resid(pieces, w, layer, h, tm, tn, name):
    m, d = h.shape
    k = w.shape[1]
    tm = min(tm, m)
    assert sum(p.shape[1] for p in pieces) == k
    in_specs = [pl.BlockSpec((tm, p.shape[1]), lambda j, i: (i, 0)) for p in pieces]
    in_specs += [_layer_cols(layer, k, tn, lambda j, i: j),
                 pl.BlockSpec((tm, tn), lambda j, i: (i, j))]
    return pl.pallas_call(
        functools.partial(_resid_body, npieces=len(pieces)),
        grid=(d // tn, m // tm),
        in_specs=in_specs,
        out_specs=pl.BlockSpec((tm, tn), lambda j, i: (i, j)),
        out_shape=jax.ShapeDtypeStruct((m, d), F32),
        scratch_shapes=[pltpu.VMEM((k, tn), BF16)],
        compiler_params=_params("parallel", "arbitrary"),
        name=name,
    )(*pieces, w, h)


def _xa_body(hn_ref, wq_ref, k_ref, v_ref, o_ref):
    q = _dot(hn_ref[...], wq_ref[...].astype(BF16)).astype(BF16)
    s = _dot_nt(q, k_ref[...]) * (XA_HEAD_DIM ** -0.5)
    m = jnp.max(s, axis=-1, keepdims=True)
    p = jnp.exp(s - m)
    denom = jnp.sum(p, axis=-1, keepdims=True)
    o = _dot(p.astype(BF16), v_ref[...]) / denom
    o_ref[...] = o.astype(o_ref.dtype)


def _cross_attention(hn, wq, layer, mem_k, mem_v, tm):
    m, d = hn.shape
    tm = min(tm, m)
    hd = XA_HEAD_DIM
    return pl.pallas_call(
        _xa_body,
        grid=(m // tm, XA_HEADS),
        in_specs=[pl.BlockSpec((tm, d), lambda i, j: (i, 0)),
                  _layer_cols(layer, d, hd, lambda i, j: j),
                  pl.BlockSpec((N_MEM, hd), lambda i, j: (0, j)),
                  pl.BlockSpec((N_MEM, hd), lambda i, j: (0, j))],
        out_specs=pl.BlockSpec((tm, hd), lambda i, j: (i, j)),
        out_shape=jax.ShapeDtypeStruct((m, d), BF16),
        compiler_params=_params("parallel", "arbitrary"),
        name="cross_attention",
    )(hn, wq, mem_k, mem_v)


def _ffn_up_body(h_ref, halo_ref, g_ref, wg_ref, wv_ref, cg_ref, cv_ref, bg_ref, bv_ref, o_ref, hn_ref):
    tm = h_ref.shape[0]

    @pl.when(pl.program_id(1) == 0)
    def _():
        gain = g_ref[...]
        halo = jnp.where(pl.program_id(0) > 0, _normalize(halo_ref[...], gain), 0.0)
        hn_ref[0:BF16_ROWS, :] = halo.astype(BF16)
        hn_ref[BF16_ROWS:, :] = _normalize(h_ref[...], gain).astype(BF16)

    a = hn_ref[BF16_ROWS:, :]
    halo = hn_ref[0:BF16_ROWS, :]

    def conv_branch(w_ref, c_ref, b_ref):
        w = w_ref[...].astype(BF16)
        u = _dot(a, w)
        uh = _dot(halo, w)
        ext = jnp.concatenate([uh[8:], u], axis=0)
        c = c_ref[...]
        return (c[2:3] * u + c[1:2] * ext[7:7 + tm] + c[0:1] * ext[6:6 + tm]) + b_ref[...]

    g = conv_branch(wg_ref, cg_ref, bg_ref)
    val = conv_branch(wv_ref, cv_ref, bv_ref)
    o_ref[...] = (_silu(g) * val).astype(o_ref.dtype)


def _ffn_up(h, gain, w_up, conv_w, conv_b, layer, tm, tn):
    m, d = h.shape
    tm = min(tm, m)
    nj = D_FF // tn
    halo_blocks = tm // BF16_ROWS

    def vec(rows, col_of):
        return pl.BlockSpec((None, rows, tn), lambda i, j: (layer, 0, col_of(j)))

    return pl.pallas_call(
        _ffn_up_body,
        grid=(m // tm, nj),
        in_specs=[pl.BlockSpec((tm, d), lambda i, j: (i, 0)),
                  pl.BlockSpec((BF16_ROWS, d), lambda i, j: (jnp.maximum(i * halo_blocks - 1, 0), 0)),
                  pl.BlockSpec((1, d), lambda i, j: (0, 0)),
                  _layer_cols(layer, d, tn, lambda i, j: j),
                  _layer_cols(layer, d, tn, lambda i, j: nj + j),
                  vec(SC_WIDTH, lambda j: j), vec(SC_WIDTH, lambda j: nj + j),
                  vec(1, lambda j: j), vec(1, lambda j: nj + j)],
        out_specs=pl.BlockSpec((tm, tn), lambda i, j: (i, j)),
        out_shape=jax.ShapeDtypeStruct((m, D_FF), BF16),
        scratch_shapes=[pltpu.VMEM((tm + BF16_ROWS, d), BF16)],
        compiler_params=_params("parallel", "arbitrary"),
        name="ffn_up",
    )(h, h, gain.reshape(1, d), w_up, w_up, conv_w, conv_w,
      conv_b.reshape(conv_b.shape[0], 1, -1), conv_b.reshape(conv_b.shape[0], 1, -1))


SWA_STEP = 2 * SWA_WINDOW


def _mixer_body(sink_ref, q0_ref, q1_ref, kc_ref, kp_ref, vc_ref, vp_ref,
                cosc_ref, sinc_ref, cosp_ref, sinp_ref,
                cb_ref, cc_ref, ch_ref, hcc_ref, hch_ref, cw_ref,
                gq_ref, gk_ref, gv_ref, gr_ref, lr_ref, wg_ref, bg_ref, gn_ref, sel_ref, ex_ref, bd_ref,
                h_ref, wo_ref, gx_ref, o_ref, hn_ref, mix_ref, st_ref, la_ref, *, last_tile):
    w = SWA_WINDOW

    @pl.when(pl.program_id(0) == 0)
    def _():
        mix_ref[...] = jnp.zeros_like(mix_ref)
        st_ref[...] = jnp.zeros_like(st_ref)

    h1 = h_ref[...] + _dot(mix_ref[...], wo_ref[...])
    o_ref[...] = h1
    hn_ref[...] = _normalize(h1, gx_ref[...]).astype(hn_ref.dtype)

    _gla_tile(gq_ref, gk_ref, gv_ref, gr_ref, lr_ref, wg_ref, bg_ref, gn_ref, sel_ref, ex_ref, bd_ref,
              st_ref, la_ref, mix_ref)

    n = jnp.minimum(pl.program_id(0), last_tile)
    lane = lax.broadcasted_iota(jnp.int32, (1, LANES), 1)
    low_half = lane < SWA_HEAD_DIM
    first_rot = (lane & (SWA_HEAD_DIM - 1)) < SWA_HEAD_DIM // 2

    def rope(x, cos, sin):
        half = SWA_HEAD_DIM // 2
        swapped = jnp.where(first_rot, pltpu.roll(x, LANES - half, 1), pltpu.roll(x, half, 1))
        return x * cos + swapped * sin

    cosc, sinc = cosc_ref[...], sinc_ref[...]
    k_cur = rope(kc_ref[...].astype(F32), cosc, sinc)
    k_prev = rope(kp_ref[...].astype(F32), cosp_ref[...], sinp_ref[...])
    v_cur = vc_ref[...].astype(F32)
    v_prev = vp_ref[...].astype(F32)

    qi = lax.broadcasted_iota(jnp.int32, (w, 2 * w), 0)
    ki = lax.broadcasted_iota(jnp.int32, (w, 2 * w), 1)
    in_window = (ki > qi) & (ki <= qi + w)
    row2 = lax.broadcasted_iota(jnp.int32, (2 * w, 1), 0)

    pairs = SWA_GROUP // LANES
    for b in range(SWA_STEP // w):
        rows = slice(b * w, (b + 1) * w)
        if b == 0:
            k_all = jnp.concatenate([k_prev, k_cur[rows]], axis=0)
            v_all = jnp.concatenate([v_prev, v_cur[rows]], axis=0)
            allowed = in_window & ((n > 0) | (ki >= w))
        else:
            k_all = k_cur[(b - 1) * w:(b + 1) * w]
            v_all = v_cur[(b - 1) * w:(b + 1) * w]
            allowed = in_window
        bias = jnp.where(allowed, 0.0, NEG_BIG)
        bias2 = jnp.concatenate([bias, bias], axis=0)
        k_rot = pltpu.roll(k_all, SWA_HEAD_DIM, 1)
        v_rot = pltpu.roll(v_all, SWA_HEAD_DIM, 1)
        for g, q_ref in enumerate((q0_ref, q1_ref)):
            if g == 0:
                kg = jnp.where(low_half, k_all, k_rot).astype(BF16)
                vg = jnp.where(low_half, v_all, v_rot).astype(BF16)
            else:
                kg = jnp.where(low_half, k_rot, k_all).astype(BF16)
                vg = jnp.where(low_half, v_rot, v_all).astype(BF16)
            for p in range(pairs):
                qp = rope(q_ref[rows, p * LANES:(p + 1) * LANES].astype(F32), cosc[rows], sinc[rows])
                qp = qp * (SWA_HEAD_DIM ** -0.5)
                stack = jnp.concatenate([jnp.where(low_half, qp, 0.0), jnp.where(low_half, 0.0, qp)], axis=0)
                s = _dot_nt(stack.astype(BF16), kg) + bias2
                head = g * (SWA_Q_HEADS // SWA_KV_HEADS) + 2 * p
                sink = jnp.where(row2 < w, sink_ref[head], sink_ref[head + 1])
                mx = jnp.maximum(jnp.max(s, axis=-1, keepdims=True), sink)
                pr = jnp.exp(s - mx)
                denom = jnp.sum(pr, axis=-1, keepdims=True) + jnp.exp(sink - mx)
                o = _dot(pr.astype(BF16), vg) / denom
                col = GLA_V + (g * pairs + p) * LANES
                mix_ref[rows, col:col + LANES] = jnp.where(low_half, o[:w], o[w:]).astype(mix_ref.dtype)

    u = cc_ref[...].astype(F32) * ch_ref[...].astype(F32)
    uh = hcc_ref[...].astype(F32) * hch_ref[...].astype(F32)
    uh = jnp.where(n > 0, uh, 0.0)
    ext = jnp.concatenate([uh[8:], u], axis=0)
    cw = cw_ref[...]
    y = cw[2:3] * u + cw[1:2] * ext[7:7 + SWA_STEP] + cw[0:1] * ext[6:6 + SWA_STEP]
    mix_ref[:, GLA_V + SWA_Q:] = (cb_ref[...].astype(F32) * y).astype(mix_ref.dtype)


def _mixer(z, h, w_out, layer, cos_t, sin_t, sinks, conv_w, w_gate, b_gate, gla_gain, next_gain):
    t, d = h.shape
    w, st = SWA_WINDOW, SWA_STEP
    nt = t // st
    sel, expand, bd_mask = _gla_constants()
    gain_t = jnp.tile(gla_gain.reshape(1, GLA_DV), (1, GLA_HEADS))
    b_gate = b_gate.reshape(1, GLA_QK)

    def whole(a):
        return pl.BlockSpec(a.shape, lambda i: (0, 0))

    def tile(i):
        return jnp.minimum(i, nt - 1)

    def cur(width, off):
        return pl.BlockSpec((st, width), lambda i: (tile(i), off // width))

    def prev(width, off):
        return pl.BlockSpec((w, width), lambda i: (jnp.maximum(tile(i) * (st // w) - 1, 0), off // width))

    def halo(off):
        return pl.BlockSpec((BF16_ROWS, SC_CH),
                            lambda i: (jnp.maximum(tile(i) * (st // BF16_ROWS) - 1, 0), off // SC_CH))

    def lagged(width):
        return pl.BlockSpec((st, width), lambda i: (jnp.maximum(i - 1, 0), 0))

    tab_c = pl.BlockSpec((st, LANES), lambda i: (tile(i), 0))
    tab_p = pl.BlockSpec((w, LANES), lambda i: (jnp.maximum(tile(i) * (st // w) - 1, 0), 0))
    return pl.pallas_call(
        functools.partial(_mixer_body, last_tile=nt - 1),
        grid=(nt + 1,),
        in_specs=[pl.BlockSpec(memory_space=pltpu.SMEM),
                  cur(SWA_GROUP, Z_SQ), cur(SWA_GROUP, Z_SQ + SWA_GROUP),
                  cur(SWA_KV, Z_SK), prev(SWA_KV, Z_SK), cur(SWA_KV, Z_SV), prev(SWA_KV, Z_SV),
                  tab_c, tab_c, tab_p, tab_p,
                  cur(SC_CH, Z_CB), cur(SC_CH, Z_CC), cur(SC_CH, Z_CH), halo(Z_CC), halo(Z_CH),
                  pl.BlockSpec((SC_WIDTH, SC_CH), lambda i: (0, 0)),
                  cur(GLA_QK, Z_Q), cur(GLA_QK, Z_K), cur(GLA_V, Z_V), cur(GLA_V, Z_R), cur(LANES, Z_LR),
                  whole(w_gate), whole(b_gate), whole(gain_t), whole(sel), whole(expand), whole(bd_mask),
                  lagged(d),
                  pl.BlockSpec((None, d, d), lambda i: (layer, 0, 0)),
                  pl.BlockSpec((1, d), lambda i: (0, 0))],
        out_specs=[lagged(d), lagged(d)],
        out_shape=[jax.ShapeDtypeStruct((t, d), F32), jax.ShapeDtypeStruct((t, d), BF16)],
        scratch_shapes=[pltpu.VMEM((st, d), BF16),
                        pltpu.VMEM((GLA_V, GLA_QK), F32), pltpu.VMEM((st, GLA_QK), F32)],
        compiler_params=_params("arbitrary"),
        name="mixer",
    )(sinks, z, z, z, z, z, z, cos_t, sin_t, cos_t, sin_t, z, z, z, z, z, conv_w,
      z, z, z, z, z, w_gate, b_gate, gain_t, sel, expand, bd_mask, h, w_out, next_gain.reshape(1, d))


def _gla_constants():
    c, r = GLA_CHUNK, GLA_SUB
    t = np.arange(c)[:, None]
    s = np.arange(c)[None, :]
    blk_t, blk_s = t // r, s // r
    mats = [
        (s <= t),
        (s > t),
        (blk_s == blk_t) & (s <= t) & (s > blk_t * r),
        (blk_s == blk_t) & (s > t),
    ]
    for i in range(1, GLA_NSUB):
        mats.append((blk_t < i) & (s > blk_t * r + r - 1) & (s <= i * r))
    sel = np.concatenate(mats, axis=0).astype(np.float32)
    d_head = np.arange(GLA_QK)[:, None] // GLA_DK
    e_head = np.arange(GLA_V)[None, :] // GLA_DV
    expand = (d_head == e_head).astype(np.float32)
    return jnp.asarray(sel, BF16), jnp.asarray(expand, BF16), jnp.asarray(expand.T, F32)


def _gla_tile(q_ref, k_ref, v_ref, r_ref, lr_ref, wg_ref, bg_ref, gn_ref, sel_ref, ex_ref, bd_ref,
              st_ref, la_ref, o_ref):
    c, r = GLA_CHUNK, GLA_SUB
    tc = q_ref.shape[0]

    pre = _dot(lr_ref[...], wg_ref[...]) + bg_ref[...]
    la_ref[...] = (jnp.minimum(pre, 0.0) - jnp.log(1.0 + jnp.exp(-jnp.abs(pre)))) * (1.0 / GLA_GATE_TAU)

    lane_qk = lax.broadcasted_iota(jnp.int32, (r, GLA_QK), 1)
    lane_v = lax.broadcasted_iota(jnp.int32, (r, GLA_V), 1)
    row_sub = lax.broadcasted_iota(jnp.int32, (r, GLA_QK), 0)
    row_up = lax.broadcasted_iota(jnp.int32, (r // 2, GLA_QK), 0) + r // 2
    col_chunk = lax.broadcasted_iota(jnp.int32, (c, c), 1)

    def head_rows(x, lane, width):
        return jnp.concatenate(
            [jnp.where((lane >= h * width) & (lane < (h + 1) * width), x, 0.0) for h in range(GLA_HEADS)],
            axis=0)

    def head_diag(x, lane, width):
        out = jnp.where(lane < width, x[0:r], 0.0)
        for h in range(1, GLA_HEADS):
            out = out + jnp.where((lane >= h * width) & (lane < (h + 1) * width), x[h * r:(h + 1) * r], 0.0)
        return out

    def chunk(ci, carry):
        r0 = pl.multiple_of(ci * c, c)
        rows = pl.ds(r0, c)
        g = la_ref[rows, :]
        g_hi = g.astype(BF16)
        g_lo = (g - g_hi.astype(F32)).astype(BF16)
        cs = _dot(sel_ref[...], jnp.concatenate([g_hi, g_lo], axis=1))
        cs = cs[:, :GLA_QK] + cs[:, GLA_QK:]
        e_b = jnp.exp(cs[0:c])
        e_tail = jnp.exp(cs[c:2 * c])
        bw = cs[2 * c:3 * c]
        e_q = jnp.exp(bw)
        e_k = jnp.exp(cs[3 * c:4 * c])

        q = q_ref[rows, :].astype(F32) * (GLA_DK ** -0.5)
        k = k_ref[rows, :].astype(F32)
        v = v_ref[rows, :]
        vf = v.astype(F32)

        st = st_ref[...]
        o_inter = _dot_nt((q * e_b).astype(BF16), st.astype(BF16))
        upd = _dot_tn(v, (k * e_tail).astype(BF16))
        st_ref[...] = st * e_b[c - 1:c, :] + upd * bd_ref[...]

        qs = q * e_q
        ks = k * e_k
        atts = []
        for i in range(1, GLA_NSUB):
            ki = (ks * jnp.exp(cs[(3 + i) * c:(4 + i) * c])).astype(BF16)
            qst = head_rows(qs[i * r:(i + 1) * r], lane_qk, GLA_DK).astype(BF16)
            att = _dot_nt(qst, ki)
            atts.append(jnp.where(col_chunk < i * r, att, 0.0))
        ov = _dot(jnp.concatenate(atts, axis=0).astype(BF16), v)

        outs = []
        for i in range(GLA_NSUB):
            sub = slice(i * r, (i + 1) * r)
            qi, ki, bwi, vi = q[sub], k[sub], bw[sub], vf[sub]
            hr = r // 2
            upper = slice(i * r + hr, (i + 1) * r)
            q_up, bw_up = q[upper], bw[upper]
            ps = []
            for s in range(r):
                if s < hr:
                    e = jnp.exp(jnp.where(row_sub >= s, bwi - bwi[s:s + 1, :], NEG_BIG))
                    ps.append((qi * ki[s:s + 1, :]) * e)
                else:
                    e = jnp.exp(jnp.where(row_up >= s, bw_up - bwi[s:s + 1, :], NEG_BIG))
                    ps.append((q_up * ki[s:s + 1, :]) * e)
            rep = _dot(jnp.concatenate(ps, axis=0).astype(BF16), ex_ref[...])
            o_lo = o_inter[i * r:i * r + hr]
            o_hi = o_inter[i * r + hr:(i + 1) * r]
            for s in range(hr):
                o_lo = o_lo + rep[s * r:s * r + hr] * vi[s:s + 1, :]
                o_hi = o_hi + rep[s * r + hr:(s + 1) * r] * vi[s:s + 1, :]
            for s in range(hr, r):
                p0 = hr * r + (s - hr) * hr
                o_hi = o_hi + rep[p0:p0 + hr] * vi[s:s + 1, :]
            o_i = jnp.concatenate([o_lo, o_hi], axis=0)
            if i > 0:
                o_i = o_i + head_diag(ov[(i - 1) * GLA_HEADS * r:i * GLA_HEADS * r], lane_v, GLA_DV)
            outs.append(o_i)
        o = jnp.concatenate(outs, axis=0)

        normed = []
        for h in range(GLA_HEADS):
            oh = o[:, h * GLA_DV:(h + 1) * GLA_DV]
            normed.append(oh * lax.rsqrt(jnp.mean(oh * oh, axis=-1, keepdims=True) + EPS))
        y = jnp.concatenate(normed, axis=1) * gn_ref[...]
        o_ref[rows, 0:GLA_V] = (y * _silu(r_ref[rows, :].astype(F32))).astype(o_ref.dtype)
        return carry

    lax.fori_loop(0, tc // c, chunk, 0, unroll=True)


def _rope_tables(positions):
    inv = 1.0 / (ROPE_THETA ** (jnp.arange(0, SWA_HEAD_DIM, 2, dtype=F32) / SWA_HEAD_DIM))
    ang = positions.astype(F32)[:, None] * inv
    cos, sin = jnp.cos(ang), jnp.sin(ang)
    reps = LANES // SWA_HEAD_DIM
    return (jnp.tile(jnp.concatenate([cos, cos], axis=-1), (1, reps)),
            jnp.tile(jnp.concatenate([-sin, sin], axis=-1), (1, reps)))


def kernel(x, mem, positions, norm_mix, w_in, gla_w_gate, gla_b_gate, gla_norm, swa_sinks, sc_conv, w_out,
           norm_x, norm_mem, xa_wq, xa_wk, xa_wv, xa_wo, norm_ffn, ffn_w_up, ffn_conv, ffn_conv_b, ffn_w_down,
           norm_final):
    assert x.shape[0] == 1 and mem.shape[0] == 1
    depth = w_in.shape[0]
    h = x[0]
    m = mem[0]
    cos_t, sin_t = _rope_tables(positions[0])
    w_in_z = _reorder_w_in(jnp.swapaxes(w_in, 1, 2), 256)
    w_out_b = _cast_bf16(w_out, 512)
    w_gate = jnp.pad(gla_w_gate, ((0, 0), (0, LANES - GLA_GATE_RANK), (0, 0))).astype(BF16)

    for l in range(depth):
        z = _norm_matmul(h, norm_mix[l], w_in_z, l, 1024, 896, "in_proj")
        h, hn = _mixer(z, h, w_out_b, l, cos_t, sin_t, swa_sinks[l], sc_conv[l],
                       w_gate[l], gla_b_gate[l], gla_norm[l], norm_x[l])
        mem_k = _norm_matmul(m, norm_mem[l], xa_wk, l, N_MEM, 512, "mem_k")
        mem_v = _norm_matmul(m, norm_mem[l], xa_wv, l, N_MEM, 512, "mem_v")
        o_x = _cross_attention(hn, xa_wq, l, mem_k, mem_v, 1024)
        h = _matmul_resid([o_x], xa_wo, l, h, 512, 1024, "xa_out_proj")
        act = _ffn_up(h, norm_ffn[l], ffn_w_up, ffn_conv, ffn_conv_b, l, 1024, 512)
        h = _matmul_resid([act], ffn_w_down, l, h, 512, 512, "ffn_down")
    return _rmsnorm(h, norm_final, F32, 512)[None]
```

```python
import functools

import numpy as np
import jax
import jax.numpy as jnp
from jax import lax
from jax.experimental import pallas as pl
from jax.experimental.pallas import tpu as pltpu

F32 = jnp.float32
BF16 = jnp.bfloat16

D_MODEL = 2048
N_MEM = 256
XA_HEADS = 4
XA_HEAD_DIM = D_MODEL // XA_HEADS
GLA_HEADS = 4
GLA_DK = 64
GLA_DV = 128
GLA_GATE_RANK = 16
GLA_GATE_TAU = 16.0
SWA_Q_HEADS = 16
SWA_KV_HEADS = 2
SWA_HEAD_DIM = 64
SWA_WINDOW = 128
ROPE_THETA = 10000.0
SC_CH = 512
SC_WIDTH = 3
D_FF = 5632
EPS = 1e-6

GLA_QK = GLA_HEADS * GLA_DK
GLA_V = GLA_HEADS * GLA_DV
SWA_Q = SWA_Q_HEADS * SWA_HEAD_DIM
SWA_KV = SWA_KV_HEADS * SWA_HEAD_DIM
SWA_GROUP = SWA_Q // SWA_KV_HEADS
IN_SIZES = (GLA_QK, GLA_QK, GLA_V, GLA_V, GLA_GATE_RANK, SWA_Q, SWA_KV, SWA_KV, SC_CH, SC_CH, SC_CH)
N_IN = sum(IN_SIZES)

LANES = 128
BF16_ROWS = 16
VMEM_LIMIT = 56 * 1024 * 1024

Z_Q, Z_K, Z_V, Z_R = 0, 256, 512, 1024
Z_SQ, Z_CB, Z_CC, Z_CH = 1536, 2560, 3072, 3584
Z_SK, Z_SV, Z_LR = 4096, 4224, 4352
Z_COLS = 4480

GLA_CHUNK = 64
GLA_SUB = 16
GLA_NSUB = GLA_CHUNK // GLA_SUB
NEG_BIG = -1e30


def _params(*sem):
    return pltpu.CompilerParams(dimension_semantics=sem, vmem_limit_bytes=VMEM_LIMIT)


def _dot(a, b):
    return jnp.dot(a, b, preferred_element_type=F32)


def _dot_nt(a, b):
    return lax.dot_general(a, b, (((1,), (1,)), ((), ())), preferred_element_type=F32)


def _dot_tn(a, b):
    return lax.dot_general(a, b, (((0,), (0,)), ((), ())), preferred_element_type=F32)


def _silu(x):
    return x / (1.0 + jnp.exp(-x))


def _normalize(x, gain):
    return x * lax.rsqrt(jnp.mean(x * x, axis=-1, keepdims=True) + EPS) * gain


def _layer_cols(layer, k, tn, col_of):
    return pl.BlockSpec((None, k, tn), lambda *ids: (layer, 0, col_of(*ids)))


def _rmsnorm_body(x_ref, g_ref, o_ref):
    o_ref[...] = _normalize(x_ref[...], g_ref[...]).astype(o_ref.dtype)


def _rmsnorm(x, gain, out_dtype, tm):
    m, d = x.shape
    tm = min(tm, m)
    return pl.pallas_call(
        _rmsnorm_body,
        grid=(m // tm,),
        in_specs=[pl.BlockSpec((tm, d), lambda i: (i, 0)),
                  pl.BlockSpec((1, d), lambda i: (0, 0))],
        out_specs=pl.BlockSpec((tm, d), lambda i: (i, 0)),
        out_shape=jax.ShapeDtypeStruct((m, d), out_dtype),
        compiler_params=_params("parallel"),
        name="rmsnorm",
    )(x, gain.reshape(1, d))


def _reorder_body(wt_ref, o_ref):
    o = np.cumsum((0,) + IN_SIZES)
    wt = wt_ref[...]
    pad = jnp.zeros((LANES - GLA_GATE_RANK, wt.shape[1]), wt.dtype)
    zt = jnp.concatenate(
        [wt[o[0]:o[4]], wt[o[5]:o[6]], wt[o[8]:o[11]], wt[o[6]:o[8]], wt[o[4]:o[5]], pad], axis=0)
    o_ref[...] = zt.T.astype(o_ref.dtype)


def _reorder_w_in(w_in_t, tk):
    depth, n, d = w_in_t.shape
    return pl.pallas_call(
        _reorder_body,
        grid=(depth, d // tk),
        in_specs=[pl.BlockSpec((None, n, tk), lambda l, i: (l, 0, i))],
        out_specs=pl.BlockSpec((None, tk, Z_COLS), lambda l, i: (l, i, 0)),
        out_shape=jax.ShapeDtypeStruct((depth, d, Z_COLS), BF16),
        compiler_params=_params("parallel", "parallel"),
        name="reorder_w_in",
    )(w_in_t)


def _cast_body(w_ref, o_ref):
    o_ref[...] = w_ref[...].astype(o_ref.dtype)


def _cast_bf16(w, tk):
    depth, k, n = w.shape
    spec = pl.BlockSpec((None, tk, n), lambda l, i: (l, i, 0))
    return pl.pallas_call(
        _cast_body,
        grid=(depth, k // tk),
        in_specs=[spec],
        out_specs=spec,
        out_shape=jax.ShapeDtypeStruct(w.shape, BF16),
        compiler_params=_params("parallel", "parallel"),
        name="cast_bf16",
    )(w)


def _mem_kv_body(m_ref, g_ref, wk_ref, wv_ref, k_ref, v_ref, mn_ref):
    @pl.when(pl.program_id(1) == 0)
    def _():
        mn_ref[...] = _normalize(m_ref[...], g_ref[...]).astype(BF16)

    mn = mn_ref[...]
    k_ref[...] = _dot(mn, wk_ref[...].astype(BF16)).astype(k_ref.dtype)
    v_ref[...] = _dot(mn, wv_ref[...].astype(BF16)).astype(v_ref.dtype)


def _mem_kv(mem, gains, wk, wv, tn):
    n_mem, d = mem.shape
    depth = wk.shape[0]
    w_spec = pl.BlockSpec((None, d, tn), lambda l, j: (l, 0, j))
    o_spec = pl.BlockSpec((None, n_mem, tn), lambda l, j: (l, 0, j))
    out = jax.ShapeDtypeStruct((depth, n_mem, d), BF16)
    return pl.pallas_call(
        _mem_kv_body,
        grid=(depth, d // tn),
        in_specs=[pl.BlockSpec((n_mem, d), lambda l, j: (0, 0)),
                  pl.BlockSpec((None, 1, d), lambda l, j: (l, 0, 0)),
                  w_spec, w_spec],
        out_specs=[o_spec, o_spec],
        out_shape=[out, out],
        scratch_shapes=[pltpu.VMEM((n_mem, d), BF16)],
        compiler_params=_params("parallel", "arbitrary"),
        name="mem_kv",
    )(mem, gains.reshape(depth, 1, d), wk, wv)


def _resid_body(a_ref, w_ref, h_ref, o_ref, wb_ref):
    @pl.when(pl.program_id(1) == 0)
    def _():
        wb_ref[...] = w_ref[...].astype(BF16)

    o_ref[...] = h_ref[...] + _dot(a_ref[...], wb_ref[...])


def _matmul_resid(a, w, layer, h, tm, tn, w_buffers, name):
    m, d = h.shape
    k = w.shape[1]
    tm = min(tm, m)
    assert a.shape == (m, k)
    return pl.pallas_call(
        _resid_body,
        grid=(d // tn, m // tm),
        in_specs=[pl.BlockSpec((tm, k), lambda j, i: (i, 0)),
                  pl.BlockSpec((None, k, tn), lambda j, i: (layer, 0, j), pipeline_mode=pl.Buffered(w_buffers)),
                  pl.BlockSpec((tm, tn), lambda j, i: (i, j))],
        out_specs=pl.BlockSpec((tm, tn), lambda j, i: (i, j)),
        out_shape=jax.ShapeDtypeStruct((m, d), F32),
        scratch_shapes=[pltpu.VMEM((k, tn), BF16)],
        compiler_params=_params("parallel", "arbitrary"),
        name=name,
    )(a, w, h)


def _xa_body(hn_ref, wq_ref, k_ref, v_ref, o_ref):
    q = _dot(hn_ref[...], wq_ref[...].astype(BF16)).astype(BF16)
    s = _dot_nt(q, k_ref[...]) * (XA_HEAD_DIM ** -0.5)
    m = jnp.max(s, axis=-1, keepdims=True)
    p = jnp.exp(s - m)
    denom = jnp.sum(p, axis=-1, keepdims=True)
    o = _dot(p.astype(BF16), v_ref[...]) / denom
    o_ref[...] = o.astype(o_ref.dtype)


def _cross_attention(hn, wq, layer, mem_k, mem_v, tm):
    m, d = hn.shape
    tm = min(tm, m)
    hd = XA_HEAD_DIM
    return pl.pallas_call(
        _xa_body,
        grid=(m // tm, XA_HEADS),
        in_specs=[pl.BlockSpec((tm, d), lambda i, j: (i, 0)),
                  _layer_cols(layer, d, hd, lambda i, j: j),
                  _layer_cols(layer, N_MEM, hd, lambda i, j: j),
                  _layer_cols(layer, N_MEM, hd, lambda i, j: j)],
        out_specs=pl.BlockSpec((tm, hd), lambda i, j: (i, j)),
        out_shape=jax.ShapeDtypeStruct((m, d), BF16),
        compiler_params=_params("parallel", "arbitrary"),
        name="cross_attention",
    )(hn, wq, mem_k, mem_v)


def _ffn_up_body(h_ref, halo_ref, g_ref, wg_ref, wv_ref, cg_ref, cv_ref, bg_ref, bv_ref, o_ref, hn_ref):
    tm = h_ref.shape[0]

    @pl.when(pl.program_id(1) == 0)
    def _():
        gain = g_ref[...]
        halo = jnp.where(pl.program_id(0) > 0, _normalize(halo_ref[...], gain), 0.0)
        hn_ref[0:BF16_ROWS, :] = halo.astype(BF16)
        hn_ref[BF16_ROWS:, :] = _normalize(h_ref[...], gain).astype(BF16)

    a = hn_ref[BF16_ROWS:, :]
    halo = hn_ref[0:BF16_ROWS, :]

    def conv_branch(w_ref, c_ref, b_ref):
        w = w_ref[...].astype(BF16)
        u = _dot(a, w)
        uh = _dot(halo, w)
        ext = jnp.concatenate([uh[8:], u], axis=0)
        c = c_ref[...]
        return (c[2:3] * u + c[1:2] * ext[7:7 + tm] + c[0:1] * ext[6:6 + tm]) + b_ref[...]

    g = conv_branch(wg_ref, cg_ref, bg_ref)
    val = conv_branch(wv_ref, cv_ref, bv_ref)
    o_ref[...] = (_silu(g) * val).astype(o_ref.dtype)


def _ffn_up(h, gain, w_up, conv_w, conv_b, layer, tm, tn):
    m, d = h.shape
    tm = min(tm, m)
    nj = D_FF // tn
    halo_blocks = tm // BF16_ROWS

    def vec(rows, col_of):
        return pl.BlockSpec((None, rows, tn), lambda i, j: (layer, 0, col_of(j)))

    return pl.pallas_call(
        _ffn_up_body,
        grid=(m // tm, nj),
        in_specs=[pl.BlockSpec((tm, d), lambda i, j: (i, 0)),
                  pl.BlockSpec((BF16_ROWS, d), lambda i, j: (jnp.maximum(i * halo_blocks - 1, 0), 0)),
                  pl.BlockSpec((1, d), lambda i, j: (0, 0)),
                  _layer_cols(layer, d, tn, lambda i, j: j),
                  _layer_cols(layer, d, tn, lambda i, j: nj + j),
                  vec(SC_WIDTH, lambda j: j), vec(SC_WIDTH, lambda j: nj + j),
                  vec(1, lambda j: j), vec(1, lambda j: nj + j)],
        out_specs=pl.BlockSpec((tm, tn), lambda i, j: (i, j)),
        out_shape=jax.ShapeDtypeStruct((m, D_FF), BF16),
        scratch_shapes=[pltpu.VMEM((tm + BF16_ROWS, d), BF16)],
        compiler_params=_params("parallel", "arbitrary"),
        name="ffn_up",
    )(h, h, gain.reshape(1, d), w_up, w_up, conv_w, conv_w,
      conv_b.reshape(conv_b.shape[0], 1, -1), conv_b.reshape(conv_b.shape[0], 1, -1))


SWA_STEP = 2 * SWA_WINDOW
IN_PROJ_COLS = 1792


def _mixer_body(sink_ref, cosc_ref, sinc_ref, cosp_ref, sinp_ref, cw_ref,
                wg_ref, bg_ref, gn_ref, sel_ref, ex_ref, bd_ref,
                hin_ref, gin_ref, wz_ref, hres_ref, wo_ref, gx_ref, o_ref, hn_ref,
                znew_ref, zcur_ref, kvprev_ref, halo_ref, mix_ref, st_ref, la_ref, *, last_tile):
    w = SWA_WINDOW
    s = pl.program_id(0)

    @pl.when(s == 0)
    def _():
        znew_ref[...] = jnp.zeros_like(znew_ref)
        zcur_ref[...] = jnp.zeros_like(zcur_ref)
        mix_ref[...] = jnp.zeros_like(mix_ref)

    @pl.when(s <= 1)
    def _():
        st_ref[...] = jnp.zeros_like(st_ref)

    kvprev_ref[...] = zcur_ref[w:, Z_SK:Z_SK + 2 * SWA_KV]
    halo_ref[...] = zcur_ref[SWA_STEP - BF16_ROWS:, Z_CC:Z_CC + 2 * SC_CH]
    zcur_ref[...] = znew_ref[...]

    h1 = hres_ref[...] + _dot(mix_ref[...], wo_ref[...])
    o_ref[...] = h1
    hn_ref[...] = _normalize(h1, gx_ref[...]).astype(hn_ref.dtype)

    hn_in = _normalize(hin_ref[...], gin_ref[...]).astype(BF16)
    for c0 in range(0, Z_COLS, IN_PROJ_COLS):
        c1 = min(c0 + IN_PROJ_COLS, Z_COLS)
        znew_ref[:, c0:c1] = _dot(hn_in, wz_ref[:, c0:c1]).astype(BF16)

    def zcols(off, width):
        return zcur_ref.at[:, off:off + width]

    q0_ref, q1_ref = zcols(Z_SQ, SWA_GROUP), zcols(Z_SQ + SWA_GROUP, SWA_GROUP)
    kc_ref, vc_ref = zcols(Z_SK, SWA_KV), zcols(Z_SV, SWA_KV)
    kp_ref, vp_ref = kvprev_ref.at[:, 0:SWA_KV], kvprev_ref.at[:, SWA_KV:2 * SWA_KV]
    cb_ref, cc_ref, ch_ref = zcols(Z_CB, SC_CH), zcols(Z_CC, SC_CH), zcols(Z_CH, SC_CH)
    hcc_ref, hch_ref = halo_ref.at[:, 0:SC_CH], halo_ref.at[:, SC_CH:2 * SC_CH]

    _gla_tile(zcols(Z_Q, GLA_QK), zcols(Z_K, GLA_QK), zcols(Z_V, GLA_V), zcols(Z_R, GLA_V), zcols(Z_LR, LANES),
              wg_ref, bg_ref, gn_ref, sel_ref, ex_ref, bd_ref, st_ref, la_ref, mix_ref)

    n = jnp.clip(s - 1, 0, last_tile)
    lane = lax.broadcasted_iota(jnp.int32, (1, LANES), 1)
    low_half = lane < SWA_HEAD_DIM
    first_rot = (lane & (SWA_HEAD_DIM - 1)) < SWA_HEAD_DIM // 2

    def rope(x, cos, sin):
        half = SWA_HEAD_DIM // 2
        swapped = jnp.where(first_rot, pltpu.roll(x, LANES - half, 1), pltpu.roll(x, half, 1))
        return x * cos + swapped * sin

    cosc, sinc = cosc_ref[...], sinc_ref[...]
    k_cur = rope(kc_ref[...].astype(F32), cosc, sinc)
    k_prev = rope(kp_ref[...].astype(F32), cosp_ref[...], sinp_ref[...])
    v_cur = vc_ref[...].astype(F32)
    v_prev = vp_ref[...].astype(F32)

    qi = lax.broadcasted_iota(jnp.int32, (w, 2 * w), 0)
    ki = lax.broadcasted_iota(jnp.int32, (w, 2 * w), 1)
    in_window = (ki > qi) & (ki <= qi + w)
    row2 = lax.broadcasted_iota(jnp.int32, (2 * w, 1), 0)

    pairs = SWA_GROUP // LANES
    for b in range(SWA_STEP // w):
        rows = slice(b * w, (b + 1) * w)
        if b == 0:
            k_all = jnp.concatenate([k_prev, k_cur[rows]], axis=0)
            v_all = jnp.concatenate([v_prev, v_cur[rows]], axis=0)
            allowed = in_window & ((n > 0) | (ki >= w))
        else:
            k_all = k_cur[(b - 1) * w:(b + 1) * w]
            v_all = v_cur[(b - 1) * w:(b + 1) * w]
            allowed = in_window
        bias = jnp.where(allowed, 0.0, NEG_BIG)
        bias2 = jnp.concatenate([bias, bias], axis=0)
        k_rot = pltpu.roll(k_all, SWA_HEAD_DIM, 1)
        v_rot = pltpu.roll(v_all, SWA_HEAD_DIM, 1)
        for g, q_ref in enumerate((q0_ref, q1_ref)):
            if g == 0:
                kg = jnp.where(low_half, k_all, k_rot).astype(BF16)
                vg = jnp.where(low_half, v_all, v_rot).astype(BF16)
            else:
                kg = jnp.where(low_half, k_rot, k_all).astype(BF16)
                vg = jnp.where(low_half, v_rot, v_all).astype(BF16)
            for p in range(pairs):
                qp = rope(q_ref[rows, p * LANES:(p + 1) * LANES].astype(F32), cosc[rows], sinc[rows])
                qp = qp * (SWA_HEAD_DIM ** -0.5)
                stack = jnp.concatenate([jnp.where(low_half, qp, 0.0), jnp.where(low_half, 0.0, qp)], axis=0)
                s = _dot_nt(stack.astype(BF16), kg) + bias2
                head = g * (SWA_Q_HEADS // SWA_KV_HEADS) + 2 * p
                sink = jnp.where(row2 < w, sink_ref[head], sink_ref[head + 1])
                mx = jnp.maximum(jnp.max(s, axis=-1, keepdims=True), sink)
                pr = jnp.exp(s - mx)
                denom = jnp.sum(pr, axis=-1, keepdims=True) + jnp.exp(sink - mx)
                o = _dot(pr.astype(BF16), vg) / denom
                col = GLA_V + (g * pairs + p) * LANES
                mix_ref[rows, col:col + LANES] = jnp.where(low_half, o[:w], o[w:]).astype(mix_ref.dtype)

    u = cc_ref[...].astype(F32) * ch_ref[...].astype(F32)
    uh = hcc_ref[...].astype(F32) * hch_ref[...].astype(F32)
    uh = jnp.where(n > 0, uh, 0.0)
    ext = jnp.concatenate([uh[8:], u], axis=0)
    cw = cw_ref[...]
    y = cw[2:3] * u + cw[1:2] * ext[7:7 + SWA_STEP] + cw[0:1] * ext[6:6 + SWA_STEP]
    mix_ref[:, GLA_V + SWA_Q:] = (cb_ref[...].astype(F32) * y).astype(mix_ref.dtype)


def _mixer(h, in_gain, w_z, w_out, layer, cos_t, sin_t, sinks, conv_w, w_gate, b_gate, gla_gain, next_gain):
    t, d = h.shape
    w, st = SWA_WINDOW, SWA_STEP
    nt = t // st
    assert d == GLA_V + SWA_Q + SC_CH
    sel, expand, bd_mask = _gla_constants()
    gain_t = jnp.tile(gla_gain.reshape(1, GLA_DV), (1, GLA_HEADS))
    b_gate = b_gate.reshape(1, GLA_QK)

    def whole(a):
        return pl.BlockSpec(a.shape, lambda i: (0, 0))

    def mixed(i):
        return jnp.clip(i - 1, 0, nt - 1)

    def rows(width, tile_of):
        return pl.BlockSpec((st, width), lambda i: (tile_of(i), 0))

    def resident(k, n):
        return pl.BlockSpec((None, k, n), lambda i: (layer, 0, 0), pipeline_mode=pl.Buffered(1))

    def projected(i):
        return jnp.clip(i - 2, 0, nt - 1)

    tab_c = rows(LANES, mixed)
    tab_p = pl.BlockSpec((w, LANES), lambda i: (jnp.maximum(mixed(i) * (st // w) - 1, 0), 0))
    return pl.pallas_call(
        functools.partial(_mixer_body, last_tile=nt - 1),
        grid=(nt + 2,),
        in_specs=[pl.BlockSpec(memory_space=pltpu.SMEM),
                  tab_c, tab_c, tab_p, tab_p,
                  pl.BlockSpec((SC_WIDTH, SC_CH), lambda i: (0, 0)),
                  whole(w_gate), whole(b_gate), whole(gain_t), whole(sel), whole(expand), whole(bd_mask),
                  rows(d, lambda i: jnp.minimum(i, nt - 1)), whole(in_gain.reshape(1, d)), resident(d, Z_COLS),
                  rows(d, projected), resident(d, d), whole(next_gain.reshape(1, d))],
        out_specs=[rows(d, projected), rows(d, projected)],
        out_shape=[jax.ShapeDtypeStruct((t, d), F32), jax.ShapeDtypeStruct((t, d), BF16)],
        scratch_shapes=[pltpu.VMEM((st, Z_COLS), BF16), pltpu.VMEM((st, Z_COLS), BF16),
                        pltpu.VMEM((w, 2 * SWA_KV), BF16), pltpu.VMEM((BF16_ROWS, 2 * SC_CH), BF16),
                        pltpu.VMEM((st, d), BF16),
                        pltpu.VMEM((GLA_V, GLA_QK), F32), pltpu.VMEM((st, GLA_QK), F32)],
        compiler_params=_params("arbitrary"),
        name="mixer",
    )(sinks, cos_t, sin_t, cos_t, sin_t, conv_w, w_gate, b_gate, gain_t, sel, expand, bd_mask,
      h, in_gain.reshape(1, d), w_z, h, w_out, next_gain.reshape(1, d))


def _gla_constants():
    c, r = GLA_CHUNK, GLA_SUB
    t = np.arange(c)[:, None]
    s = np.arange(c)[None, :]
    blk_t, blk_s = t // r, s // r
    mats = [
        (s <= t),
        (s > t),
        (blk_s == blk_t) & (s <= t) & (s > blk_t * r),
        (blk_s == blk_t) & (s > t),
    ]
    for i in range(1, GLA_NSUB):
        mats.append((blk_t < i) & (s > blk_t * r + r - 1) & (s <= i * r))
    sel = np.concatenate(mats, axis=0).astype(np.float32)
    d_head = np.arange(GLA_QK)[:, None] // GLA_DK
    e_head = np.arange(GLA_V)[None, :] // GLA_DV
    expand = (d_head == e_head).astype(np.float32)
    return jnp.asarray(sel, BF16), jnp.asarray(expand, BF16), jnp.asarray(expand.T, F32)


def _gla_tile(q_ref, k_ref, v_ref, r_ref, lr_ref, wg_ref, bg_ref, gn_ref, sel_ref, ex_ref, bd_ref,
              st_ref, la_ref, o_ref):
    c, r = GLA_CHUNK, GLA_SUB
    tc = q_ref.shape[0]

    pre = _dot(lr_ref[...], wg_ref[...]) + bg_ref[...]
    la_ref[...] = (jnp.minimum(pre, 0.0) - jnp.log(1.0 + jnp.exp(-jnp.abs(pre)))) * (1.0 / GLA_GATE_TAU)

    lane_qk = lax.broadcasted_iota(jnp.int32, (r, GLA_QK), 1)
    lane_v = lax.broadcasted_iota(jnp.int32, (r, GLA_V), 1)
    row_sub = lax.broadcasted_iota(jnp.int32, (r, GLA_QK), 0)
    row_up = lax.broadcasted_iota(jnp.int32, (r // 2, GLA_QK), 0) + r // 2
    col_chunk = lax.broadcasted_iota(jnp.int32, (c, c), 1)

    def head_rows(x, lane, width):
        return jnp.concatenate(
            [jnp.where((lane >= h * width) & (lane < (h + 1) * width), x, 0.0) for h in range(GLA_HEADS)],
            axis=0)

    def head_diag(x, lane, width):
        out = jnp.where(lane < width, x[0:r], 0.0)
        for h in range(1, GLA_HEADS):
            out = out + jnp.where((lane >= h * width) & (lane < (h + 1) * width), x[h * r:(h + 1) * r], 0.0)
        return out

    def chunk(ci):
        rows = pl.ds(ci * c, c)
        g = la_ref[rows, :]
        g_hi = g.astype(BF16)
        g_lo = (g - g_hi.astype(F32)).astype(BF16)
        cs = _dot(sel_ref[...], jnp.concatenate([g_hi, g_lo], axis=1))
        cs = cs[:, :GLA_QK] + cs[:, GLA_QK:]
        e_b = jnp.exp(cs[0:c])
        e_tail = jnp.exp(cs[c:2 * c])
        bw = cs[2 * c:3 * c]
        e_q = jnp.exp(bw)
        e_k = jnp.exp(cs[3 * c:4 * c])

        q = q_ref[rows, :].astype(F32) * (GLA_DK ** -0.5)
        k = k_ref[rows, :].astype(F32)
        v = v_ref[rows, :]
        vf = v.astype(F32)

        st = st_ref[...]
        o_inter = _dot_nt((q * e_b).astype(BF16), st.astype(BF16))
        upd = _dot_tn(v, (k * e_tail).astype(BF16))
        st_ref[...] = st * e_b[c - 1:c, :] + upd * bd_ref[...]

        qs = q * e_q
        ks = k * e_k
        atts = []
        for i in range(1, GLA_NSUB):
            ki = (ks * jnp.exp(cs[(3 + i) * c:(4 + i) * c])).astype(BF16)
            qst = head_rows(qs[i * r:(i + 1) * r], lane_qk, GLA_DK).astype(BF16)
            att = _dot_nt(qst, ki)
            atts.append(jnp.where(col_chunk < i * r, att, 0.0))
        ov = _dot(jnp.concatenate(atts, axis=0).astype(BF16), v)

        outs = []
        for i in range(GLA_NSUB):
            sub = slice(i * r, (i + 1) * r)
            qi, ki, bwi, vi = q[sub], k[sub], bw[sub], vf[sub]
            hr = r // 2
            upper = slice(i * r + hr, (i + 1) * r)
            q_up, bw_up = q[upper], bw[upper]
            ps = []
            for s in range(r):
                if s < hr:
                    e = jnp.exp(jnp.where(row_sub >= s, bwi - bwi[s:s + 1, :], NEG_BIG))
                    ps.append((qi * ki[s:s + 1, :]) * e)
                else:
                    e = jnp.exp(jnp.where(row_up >= s, bw_up - bwi[s:s + 1, :], NEG_BIG))
                    ps.append((q_up * ki[s:s + 1, :]) * e)
            rep = _dot(jnp.concatenate(ps, axis=0).astype(BF16), ex_ref[...])
            o_lo = o_inter[i * r:i * r + hr]
            o_hi = o_inter[i * r + hr:(i + 1) * r]
            for s in range(hr):
                o_lo = o_lo + rep[s * r:s * r + hr] * vi[s:s + 1, :]
                o_hi = o_hi + rep[s * r + hr:(s + 1) * r] * vi[s:s + 1, :]
            for s in range(hr, r):
                p0 = hr * r + (s - hr) * hr
                o_hi = o_hi + rep[p0:p0 + hr] * vi[s:s + 1, :]
            o_i = jnp.concatenate([o_lo, o_hi], axis=0)
            if i > 0:
                o_i = o_i + head_diag(ov[(i - 1) * GLA_HEADS * r:i * GLA_HEADS * r], lane_v, GLA_DV)
            outs.append(o_i)
        o = jnp.concatenate(outs, axis=0)

        normed = []
        for h in range(GLA_HEADS):
            oh = o[:, h * GLA_DV:(h + 1) * GLA_DV]
            normed.append(oh * lax.rsqrt(jnp.mean(oh * oh, axis=-1, keepdims=True) + EPS))
        y = jnp.concatenate(normed, axis=1) * gn_ref[...]
        o_ref[rows, 0:GLA_V] = (y * _silu(r_ref[rows, :].astype(F32))).astype(o_ref.dtype)
    for ci in range(tc // c):
        chunk(ci)


def _rope_tables(positions):
    inv = 1.0 / (ROPE_THETA ** (jnp.arange(0, SWA_HEAD_DIM, 2, dtype=F32) / SWA_HEAD_DIM))
    ang = positions.astype(F32)[:, None] * inv
    cos, sin = jnp.cos(ang), jnp.sin(ang)
    reps = LANES // SWA_HEAD_DIM
    return (jnp.tile(jnp.concatenate([cos, cos], axis=-1), (1, reps)),
            jnp.tile(jnp.concatenate([-sin, sin], axis=-1), (1, reps)))


def kernel(x, mem, positions, norm_mix, w_in, gla_w_gate, gla_b_gate, gla_norm, swa_sinks, sc_conv, w_out,
           norm_x, norm_mem, xa_wq, xa_wk, xa_wv, xa_wo, norm_ffn, ffn_w_up, ffn_conv, ffn_conv_b, ffn_w_down,
           norm_final):
    assert x.shape[0] == 1 and mem.shape[0] == 1
    depth = w_in.shape[0]
    h = x[0]
    m = mem[0]
    cos_t, sin_t = _rope_tables(positions[0])
    w_in_z = _reorder_w_in(jnp.swapaxes(w_in, 1, 2), 256)
    w_out_b = _cast_bf16(w_out, 512)
    w_gate = jnp.pad(gla_w_gate, ((0, 0), (0, LANES - GLA_GATE_RANK), (0, 0))).astype(BF16)
    mem_k, mem_v = _mem_kv(m, norm_mem, xa_wk, xa_wv, 512)

    for l in range(depth):
        h, hn = _mixer(h, norm_mix[l], w_in_z, w_out_b, l, cos_t, sin_t, swa_sinks[l], sc_conv[l],
                       w_gate[l], gla_b_gate[l], gla_norm[l], norm_x[l])
        o_x = _cross_attention(hn, xa_wq, l, mem_k, mem_v, 1024)
        h = _matmul_resid(o_x, xa_wo, l, h, 512, 1024, 2, "xa_out_proj")
        act = _ffn_up(h, norm_ffn[l], ffn_w_up, ffn_conv, ffn_conv_b, l, 1024, 512)
        h = _matmul_resid(act, ffn_w_down, l, h, 256, 1024, 1, "ffn_down")
    return _rmsnorm(h, norm_final, F32, 512)[None]
```

```python
import functools

import numpy as np
import jax
import jax.numpy as jnp
from jax import lax
from jax.experimental import pallas as pl
from jax.experimental.pallas import tpu as pltpu

F32 = jnp.float32
BF16 = jnp.bfloat16

D_MODEL = 2048
N_MEM = 256
XA_HEADS = 4
XA_HEAD_DIM = D_MODEL // XA_HEADS
GLA_HEADS = 4
GLA_DK = 64
GLA_DV = 128
GLA_GATE_RANK = 16
GLA_GATE_TAU = 16.0
SWA_Q_HEADS = 16
SWA_KV_HEADS = 2
SWA_HEAD_DIM = 64
SWA_WINDOW = 128
ROPE_THETA = 10000.0
SC_CH = 512
SC_WIDTH = 3
D_FF = 5632
EPS = 1e-6

GLA_QK = GLA_HEADS * GLA_DK
GLA_V = GLA_HEADS * GLA_DV
SWA_Q = SWA_Q_HEADS * SWA_HEAD_DIM
SWA_KV = SWA_KV_HEADS * SWA_HEAD_DIM
SWA_GROUP = SWA_Q // SWA_KV_HEADS
IN_SIZES = (GLA_QK, GLA_QK, GLA_V, GLA_V, GLA_GATE_RANK, SWA_Q, SWA_KV, SWA_KV, SC_CH, SC_CH, SC_CH)
N_IN = sum(IN_SIZES)

LANES = 128
BF16_ROWS = 16
VMEM_LIMIT = 56 * 1024 * 1024

Z_Q, Z_K, Z_V, Z_R = 0, 256, 512, 1024
Z_SQ, Z_CB, Z_CC, Z_CH = 1536, 2560, 3072, 3584
Z_SK, Z_SV, Z_LR = 4096, 4224, 4352
Z_COLS = 4480

GLA_CHUNK = 64
GLA_SUB = 16
GLA_NSUB = GLA_CHUNK // GLA_SUB
NEG_BIG = -1e30


def _params(*sem):
    return pltpu.CompilerParams(dimension_semantics=sem, vmem_limit_bytes=VMEM_LIMIT)


def _dot(a, b):
    return jnp.dot(a, b, preferred_element_type=F32)


def _dot_nt(a, b):
    return lax.dot_general(a, b, (((1,), (1,)), ((), ())), preferred_element_type=F32)


def _dot_tn(a, b):
    return lax.dot_general(a, b, (((0,), (0,)), ((), ())), preferred_element_type=F32)


def _silu(x):
    return x / (1.0 + jnp.exp(-x))


def _normalize(x, gain):
    return x * lax.rsqrt(jnp.mean(x * x, axis=-1, keepdims=True) + EPS) * gain


def _layer_cols(layer, k, tn, col_of):
    return pl.BlockSpec((None, k, tn), lambda *ids: (layer, 0, col_of(*ids)))


def _rmsnorm_body(x_ref, g_ref, o_ref):
    o_ref[...] = _normalize(x_ref[...], g_ref[...]).astype(o_ref.dtype)


def _rmsnorm(x, gain, out_dtype, tm):
    m, d = x.shape
    tm = min(tm, m)
    return pl.pallas_call(
        _rmsnorm_body,
        grid=(m // tm,),
        in_specs=[pl.BlockSpec((tm, d), lambda i: (i, 0)),
                  pl.BlockSpec((1, d), lambda i: (0, 0))],
        out_specs=pl.BlockSpec((tm, d), lambda i: (i, 0)),
        out_shape=jax.ShapeDtypeStruct((m, d), out_dtype),
        compiler_params=_params("parallel"),
        name="rmsnorm",
    )(x, gain.reshape(1, d))


def _reorder_body(wt_ref, o_ref):
    o = np.cumsum((0,) + IN_SIZES)
    wt = wt_ref[...]
    pad = jnp.zeros((LANES - GLA_GATE_RANK, wt.shape[1]), wt.dtype)
    zt = jnp.concatenate(
        [wt[o[0]:o[4]], wt[o[5]:o[6]], wt[o[8]:o[11]], wt[o[6]:o[8]], wt[o[4]:o[5]], pad], axis=0)
    o_ref[...] = zt.T.astype(o_ref.dtype)


def _reorder_w_in(w_in_t, tk):
    depth, n, d = w_in_t.shape
    return pl.pallas_call(
        _reorder_body,
        grid=(depth, d // tk),
        in_specs=[pl.BlockSpec((None, n, tk), lambda l, i: (l, 0, i))],
        out_specs=pl.BlockSpec((None, tk, Z_COLS), lambda l, i: (l, i, 0)),
        out_shape=jax.ShapeDtypeStruct((depth, d, Z_COLS), BF16),
        compiler_params=_params("parallel", "parallel"),
        name="reorder_w_in",
    )(w_in_t)


def _cast_body(w_ref, o_ref):
    o_ref[...] = w_ref[...].astype(o_ref.dtype)


def _cast_bf16(w, tk):
    depth, k, n = w.shape
    spec = pl.BlockSpec((None, tk, n), lambda l, i: (l, i, 0))
    return pl.pallas_call(
        _cast_body,
        grid=(depth, k // tk),
        in_specs=[spec],
        out_specs=spec,
        out_shape=jax.ShapeDtypeStruct(w.shape, BF16),
        compiler_params=_params("parallel", "parallel"),
        name="cast_bf16",
    )(w)


def _mem_kv_body(m_ref, g_ref, wk_ref, wv_ref, k_ref, v_ref, mn_ref):
    @pl.when(pl.program_id(1) == 0)
    def _():
        mn_ref[...] = _normalize(m_ref[...], g_ref[...]).astype(BF16)

    mn = mn_ref[...]
    k_ref[...] = _dot(mn, wk_ref[...].astype(BF16)).astype(k_ref.dtype)
    v_ref[...] = _dot(mn, wv_ref[...].astype(BF16)).astype(v_ref.dtype)


def _mem_kv(mem, gains, wk, wv, tn):
    n_mem, d = mem.shape
    depth = wk.shape[0]
    w_spec = pl.BlockSpec((None, d, tn), lambda l, j: (l, 0, j))
    o_spec = pl.BlockSpec((None, n_mem, tn), lambda l, j: (l, 0, j))
    out = jax.ShapeDtypeStruct((depth, n_mem, d), BF16)
    return pl.pallas_call(
        _mem_kv_body,
        grid=(depth, d // tn),
        in_specs=[pl.BlockSpec((n_mem, d), lambda l, j: (0, 0)),
                  pl.BlockSpec((None, 1, d), lambda l, j: (l, 0, 0)),
                  w_spec, w_spec],
        out_specs=[o_spec, o_spec],
        out_shape=[out, out],
        scratch_shapes=[pltpu.VMEM((n_mem, d), BF16)],
        compiler_params=_params("parallel", "arbitrary"),
        name="mem_kv",
    )(mem, gains.reshape(depth, 1, d), wk, wv)


def _resid_body(a_ref, w_ref, h_ref, o_ref, wb_ref):
    @pl.when(pl.program_id(1) == 0)
    def _():
        wb_ref[...] = w_ref[...].astype(BF16)

    o_ref[...] = h_ref[...] + _dot(a_ref[...], wb_ref[...])


def _matmul_resid(a, w, layer, h, tm, tn, w_buffers, name):
    m, d = h.shape
    k = w.shape[1]
    tm = min(tm, m)
    assert a.shape == (m, k)
    return pl.pallas_call(
        _resid_body,
        grid=(d // tn, m // tm),
        in_specs=[pl.BlockSpec((tm, k), lambda j, i: (i, 0)),
                  pl.BlockSpec((None, k, tn), lambda j, i: (layer, 0, j), pipeline_mode=pl.Buffered(w_buffers)),
                  pl.BlockSpec((tm, tn), lambda j, i: (i, j))],
        out_specs=pl.BlockSpec((tm, tn), lambda j, i: (i, j)),
        out_shape=jax.ShapeDtypeStruct((m, d), F32),
        scratch_shapes=[pltpu.VMEM((k, tn), BF16)],
        compiler_params=_params("parallel", "arbitrary"),
        name=name,
    )(a, w, h)


def _resid_norm_body(a_ref, w_ref, h_ref, g_ref, o_ref, hn_ref, wb_ref):
    @pl.when(pl.program_id(0) == 0)
    def _():
        wb_ref[...] = w_ref[...].astype(BF16)

    h = h_ref[...] + _dot(a_ref[...], wb_ref[...])
    o_ref[...] = h
    hn_ref[...] = _normalize(h, g_ref[...]).astype(hn_ref.dtype)


def _matmul_resid_norm(a, w, layer, h, gain, tm, name):
    m, d = h.shape
    k = w.shape[1]
    tm = min(tm, m)
    rows = pl.BlockSpec((tm, d), lambda i: (i, 0))
    return pl.pallas_call(
        _resid_norm_body,
        grid=(m // tm,),
        in_specs=[pl.BlockSpec((tm, k), lambda i: (i, 0)),
                  pl.BlockSpec((None, k, d), lambda i: (layer, 0, 0), pipeline_mode=pl.Buffered(1)),
                  rows,
                  pl.BlockSpec((1, d), lambda i: (0, 0))],
        out_specs=[rows, rows],
        out_shape=[jax.ShapeDtypeStruct((m, d), F32), jax.ShapeDtypeStruct((m, d), BF16)],
        scratch_shapes=[pltpu.VMEM((k, d), BF16)],
        compiler_params=_params("arbitrary"),
        name=name,
    )(a, w, h, gain.reshape(1, d))


def _xa_body(hn_ref, wq_ref, k_ref, v_ref, o_ref):
    q = _dot(hn_ref[...], wq_ref[...].astype(BF16)).astype(BF16)
    s = _dot_nt(q, k_ref[...]) * (XA_HEAD_DIM ** -0.5)
    m = jnp.max(s, axis=-1, keepdims=True)
    p = jnp.exp(s - m)
    denom = jnp.sum(p, axis=-1, keepdims=True)
    o = _dot(p.astype(BF16), v_ref[...]) / denom
    o_ref[...] = o.astype(o_ref.dtype)


def _cross_attention(hn, wq, layer, mem_k, mem_v, tm):
    m, d = hn.shape
    tm = min(tm, m)
    hd = XA_HEAD_DIM
    return pl.pallas_call(
        _xa_body,
        grid=(m // tm, XA_HEADS),
        in_specs=[pl.BlockSpec((tm, d), lambda i, j: (i, 0)),
                  _layer_cols(layer, d, hd, lambda i, j: j),
                  _layer_cols(layer, N_MEM, hd, lambda i, j: j),
                  _layer_cols(layer, N_MEM, hd, lambda i, j: j)],
        out_specs=pl.BlockSpec((tm, hd), lambda i, j: (i, j)),
        out_shape=jax.ShapeDtypeStruct((m, d), BF16),
        compiler_params=_params("parallel", "arbitrary"),
        name="cross_attention",
    )(hn, wq, mem_k, mem_v)


def _ffn_up_body(hn_ref, halo_ref, wg_ref, wv_ref, cg_ref, cv_ref, bg_ref, bv_ref, o_ref):
    tm = hn_ref.shape[0]
    a = hn_ref[...]
    halo = jnp.where(pl.program_id(0) > 0, halo_ref[...], jnp.zeros_like(halo_ref))

    def conv_branch(w_ref, c_ref, b_ref):
        w = w_ref[...].astype(BF16)
        u = _dot(a, w)
        uh = _dot(halo, w)
        ext = jnp.concatenate([uh[8:], u], axis=0)
        c = c_ref[...]
        return (c[2:3] * u + c[1:2] * ext[7:7 + tm] + c[0:1] * ext[6:6 + tm]) + b_ref[...]

    g = conv_branch(wg_ref, cg_ref, bg_ref)
    val = conv_branch(wv_ref, cv_ref, bv_ref)
    o_ref[...] = (_silu(g) * val).astype(o_ref.dtype)


def _ffn_up(hn, w_up, conv_w, conv_b, layer, tm, tn):
    m, d = hn.shape
    tm = min(tm, m)
    nj = D_FF // tn
    halo_blocks = tm // BF16_ROWS

    def vec(rows, col_of):
        return pl.BlockSpec((None, rows, tn), lambda i, j: (layer, 0, col_of(j)))

    return pl.pallas_call(
        _ffn_up_body,
        grid=(m // tm, nj),
        in_specs=[pl.BlockSpec((tm, d), lambda i, j: (i, 0)),
                  pl.BlockSpec((BF16_ROWS, d), lambda i, j: (jnp.maximum(i * halo_blocks - 1, 0), 0)),
                  _layer_cols(layer, d, tn, lambda i, j: j),
                  _layer_cols(layer, d, tn, lambda i, j: nj + j),
                  vec(SC_WIDTH, lambda j: j), vec(SC_WIDTH, lambda j: nj + j),
                  vec(1, lambda j: j), vec(1, lambda j: nj + j)],
        out_specs=pl.BlockSpec((tm, tn), lambda i, j: (i, j)),
        out_shape=jax.ShapeDtypeStruct((m, D_FF), BF16),
        compiler_params=_params("parallel", "arbitrary"),
        name="ffn_up",
    )(hn, hn, w_up, w_up, conv_w, conv_w,
      conv_b.reshape(conv_b.shape[0], 1, -1), conv_b.reshape(conv_b.shape[0], 1, -1))


SWA_STEP = 2 * SWA_WINDOW
IN_PROJ_COLS = 1792


def _mixer_body(sink_ref, cosc_ref, sinc_ref, cosp_ref, sinp_ref, cw_ref,
                wg_ref, bg_ref, gn_ref, sel_ref, ex_ref, bd_ref,
                hin_ref, gin_ref, wz_ref, hres_ref, wo_ref, gx_ref, o_ref, hn_ref,
                znew_ref, zcur_ref, kvprev_ref, halo_ref, mix_ref, st_ref, la_ref, *, last_tile):
    w = SWA_WINDOW
    s = pl.program_id(0)

    @pl.when(s == 0)
    def _():
        znew_ref[...] = jnp.zeros_like(znew_ref)
        zcur_ref[...] = jnp.zeros_like(zcur_ref)
        mix_ref[...] = jnp.zeros_like(mix_ref)

    @pl.when(s <= 1)
    def _():
        st_ref[...] = jnp.zeros_like(st_ref)

    kvprev_ref[...] = zcur_ref[w:, Z_SK:Z_SK + 2 * SWA_KV]
    halo_ref[...] = zcur_ref[SWA_STEP - BF16_ROWS:, Z_CC:Z_CC + 2 * SC_CH]
    zcur_ref[...] = znew_ref[...]

    h1 = hres_ref[...] + _dot(mix_ref[...], wo_ref[...])
    o_ref[...] = h1
    hn_ref[...] = _normalize(h1, gx_ref[...]).astype(hn_ref.dtype)

    hn_in = _normalize(hin_ref[...], gin_ref[...]).astype(BF16)
    for c0 in range(0, Z_COLS, IN_PROJ_COLS):
        c1 = min(c0 + IN_PROJ_COLS, Z_COLS)
        znew_ref[:, c0:c1] = _dot(hn_in, wz_ref[:, c0:c1]).astype(BF16)

    def zcols(off, width):
        return zcur_ref.at[:, off:off + width]

    q0_ref, q1_ref = zcols(Z_SQ, SWA_GROUP), zcols(Z_SQ + SWA_GROUP, SWA_GROUP)
    kc_ref, vc_ref = zcols(Z_SK, SWA_KV), zcols(Z_SV, SWA_KV)
    kp_ref, vp_ref = kvprev_ref.at[:, 0:SWA_KV], kvprev_ref.at[:, SWA_KV:2 * SWA_KV]
    cb_ref, cc_ref, ch_ref = zcols(Z_CB, SC_CH), zcols(Z_CC, SC_CH), zcols(Z_CH, SC_CH)
    hcc_ref, hch_ref = halo_ref.at[:, 0:SC_CH], halo_ref.at[:, SC_CH:2 * SC_CH]

    _gla_tile(zcols(Z_Q, GLA_QK), zcols(Z_K, GLA_QK), zcols(Z_V, GLA_V), zcols(Z_R, GLA_V), zcols(Z_LR, LANES),
              wg_ref, bg_ref, gn_ref, sel_ref, ex_ref, bd_ref, st_ref, la_ref, mix_ref)

    n = jnp.clip(s - 1, 0, last_tile)
    lane = lax.broadcasted_iota(jnp.int32, (1, LANES), 1)
    low_half = lane < SWA_HEAD_DIM
    first_rot = (lane & (SWA_HEAD_DIM - 1)) < SWA_HEAD_DIM // 2

    def rope(x, cos, sin):
        half = SWA_HEAD_DIM // 2
        swapped = jnp.where(first_rot, pltpu.roll(x, LANES - half, 1), pltpu.roll(x, half, 1))
        return x * cos + swapped * sin

    cosc, sinc = cosc_ref[...], sinc_ref[...]
    k_cur = rope(kc_ref[...].astype(F32), cosc, sinc)
    k_prev = rope(kp_ref[...].astype(F32), cosp_ref[...], sinp_ref[...])
    v_cur = vc_ref[...].astype(F32)
    v_prev = vp_ref[...].astype(F32)

    qi = lax.broadcasted_iota(jnp.int32, (w, 2 * w), 0)
    ki = lax.broadcasted_iota(jnp.int32, (w, 2 * w), 1)
    in_window = (ki > qi) & (ki <= qi + w)
    row2 = lax.broadcasted_iota(jnp.int32, (2 * w, 1), 0)

    pairs = SWA_GROUP // LANES
    for b in range(SWA_STEP // w):
        rows = slice(b * w, (b + 1) * w)
        if b == 0:
            k_all = jnp.concatenate([k_prev, k_cur[rows]], axis=0)
            v_all = jnp.concatenate([v_prev, v_cur[rows]], axis=0)
            allowed = in_window & ((n > 0) | (ki >= w))
        else:
            k_all = k_cur[(b - 1) * w:(b + 1) * w]
            v_all = v_cur[(b - 1) * w:(b + 1) * w]
            allowed = in_window
        bias = jnp.where(allowed, 0.0, NEG_BIG)
        bias2 = jnp.concatenate([bias, bias], axis=0)
        k_rot = pltpu.roll(k_all, SWA_HEAD_DIM, 1)
        v_rot = pltpu.roll(v_all, SWA_HEAD_DIM, 1)
        for g, q_ref in enumerate((q0_ref, q1_ref)):
            if g == 0:
                kg = jnp.where(low_half, k_all, k_rot).astype(BF16)
                vg = jnp.where(low_half, v_all, v_rot).astype(BF16)
            else:
                kg = jnp.where(low_half, k_rot, k_all).astype(BF16)
                vg = jnp.where(low_half, v_rot, v_all).astype(BF16)
            for p in range(pairs):
                qp = rope(q_ref[rows, p * LANES:(p + 1) * LANES].astype(F32), cosc[rows], sinc[rows])
                qp = qp * (SWA_HEAD_DIM ** -0.5)
                stack = jnp.concatenate([jnp.where(low_half, qp, 0.0), jnp.where(low_half, 0.0, qp)], axis=0)
                s = _dot_nt(stack.astype(BF16), kg) + bias2
                head = g * (SWA_Q_HEADS // SWA_KV_HEADS) + 2 * p
                sink = jnp.where(row2 < w, sink_ref[head], sink_ref[head + 1])
                mx = jnp.maximum(jnp.max(s, axis=-1, keepdims=True), sink)
                pr = jnp.exp(s - mx)
                denom = jnp.sum(pr, axis=-1, keepdims=True) + jnp.exp(sink - mx)
                o = _dot(pr.astype(BF16), vg) / denom
                col = GLA_V + (g * pairs + p) * LANES
                mix_ref[rows, col:col + LANES] = jnp.where(low_half, o[:w], o[w:]).astype(mix_ref.dtype)

    u = cc_ref[...].astype(F32) * ch_ref[...].astype(F32)
    uh = hcc_ref[...].astype(F32) * hch_ref[...].astype(F32)
    uh = jnp.where(n > 0, uh, 0.0)
    ext = jnp.concatenate([uh[8:], u], axis=0)
    cw = cw_ref[...]
    y = cw[2:3] * u + cw[1:2] * ext[7:7 + SWA_STEP] + cw[0:1] * ext[6:6 + SWA_STEP]
    mix_ref[:, GLA_V + SWA_Q:] = (cb_ref[...].astype(F32) * y).astype(mix_ref.dtype)


def _mixer(h, in_gain, w_z, w_out, layer, cos_t, sin_t, sinks, conv_w, w_gate, b_gate, gla_gain, next_gain):
    t, d = h.shape
    w, st = SWA_WINDOW, SWA_STEP
    nt = t // st
    assert d == GLA_V + SWA_Q + SC_CH
    sel, expand, bd_mask = _gla_constants()
    gain_t = jnp.tile(gla_gain.reshape(1, GLA_DV), (1, GLA_HEADS))
    b_gate = b_gate.reshape(1, GLA_QK)

    def whole(a):
        return pl.BlockSpec(a.shape, lambda i: (0, 0))

    def mixed(i):
        return jnp.clip(i - 1, 0, nt - 1)

    def rows(width, tile_of):
        return pl.BlockSpec((st, width), lambda i: (tile_of(i), 0))

    def resident(k, n):
        return pl.BlockSpec((None, k, n), lambda i: (layer, 0, 0), pipeline_mode=pl.Buffered(1))

    def projected(i):
        return jnp.clip(i - 2, 0, nt - 1)

    tab_c = rows(LANES, mixed)
    tab_p = pl.BlockSpec((w, LANES), lambda i: (jnp.maximum(mixed(i) * (st // w) - 1, 0), 0))
    return pl.pallas_call(
        functools.partial(_mixer_body, last_tile=nt - 1),
        grid=(nt + 2,),
        in_specs=[pl.BlockSpec(memory_space=pltpu.SMEM),
                  tab_c, tab_c, tab_p, tab_p,
                  pl.BlockSpec((SC_WIDTH, SC_CH), lambda i: (0, 0)),
                  whole(w_gate), whole(b_gate), whole(gain_t), whole(sel), whole(expand), whole(bd_mask),
                  rows(d, lambda i: jnp.minimum(i, nt - 1)), whole(in_gain.reshape(1, d)), resident(d, Z_COLS),
                  rows(d, projected), resident(d, d), whole(next_gain.reshape(1, d))],
        out_specs=[rows(d, projected), rows(d, projected)],
        out_shape=[jax.ShapeDtypeStruct((t, d), F32), jax.ShapeDtypeStruct((t, d), BF16)],
        scratch_shapes=[pltpu.VMEM((st, Z_COLS), BF16), pltpu.VMEM((st, Z_COLS), BF16),
                        pltpu.VMEM((w, 2 * SWA_KV), BF16), pltpu.VMEM((BF16_ROWS, 2 * SC_CH), BF16),
                        pltpu.VMEM((st, d), BF16),
                        pltpu.VMEM((GLA_V, GLA_QK), F32), pltpu.VMEM((st, GLA_QK), F32)],
        compiler_params=_params("arbitrary"),
        name="mixer",
    )(sinks, cos_t, sin_t, cos_t, sin_t, conv_w, w_gate, b_gate, gain_t, sel, expand, bd_mask,
      h, in_gain.reshape(1, d), w_z, h, w_out, next_gain.reshape(1, d))


def _gla_constants():
    c, r = GLA_CHUNK, GLA_SUB
    t = np.arange(c)[:, None]
    s = np.arange(c)[None, :]
    blk_t, blk_s = t // r, s // r
    mats = [
        (s <= t),
        (s > t),
        (blk_s == blk_t) & (s <= t) & (s > blk_t * r),
        (blk_s == blk_t) & (s > t),
    ]
    for i in range(1, GLA_NSUB):
        mats.append((blk_t < i) & (s > blk_t * r + r - 1) & (s <= i * r))
    sel = np.concatenate(mats, axis=0).astype(np.float32)
    d_head = np.arange(GLA_QK)[:, None] // GLA_DK
    e_head = np.arange(GLA_V)[None, :] // GLA_DV
    expand = (d_head == e_head).astype(np.float32)
    return jnp.asarray(sel, BF16), jnp.asarray(expand, BF16), jnp.asarray(expand.T, F32)


def _gla_tile(q_ref, k_ref, v_ref, r_ref, lr_ref, wg_ref, bg_ref, gn_ref, sel_ref, ex_ref, bd_ref,
              st_ref, la_ref, o_ref):
    c, r = GLA_CHUNK, GLA_SUB
    tc = q_ref.shape[0]

    pre = _dot(lr_ref[...], wg_ref[...]) + bg_ref[...]
    la_ref[...] = (jnp.minimum(pre, 0.0) - jnp.log(1.0 + jnp.exp(-jnp.abs(pre)))) * (1.0 / GLA_GATE_TAU)

    lane_qk = lax.broadcasted_iota(jnp.int32, (r, GLA_QK), 1)
    lane_v = lax.broadcasted_iota(jnp.int32, (r, GLA_V), 1)
    row_sub = lax.broadcasted_iota(jnp.int32, (r, GLA_QK), 0)
    row_up = lax.broadcasted_iota(jnp.int32, (r // 2, GLA_QK), 0) + r // 2
    col_chunk = lax.broadcasted_iota(jnp.int32, (c, c), 1)

    def head_rows(x, lane, width):
        return jnp.concatenate(
            [jnp.where((lane >= h * width) & (lane < (h + 1) * width), x, 0.0) for h in range(GLA_HEADS)],
            axis=0)

    def head_diag(x, lane, width):
        out = jnp.where(lane < width, x[0:r], 0.0)
        for h in range(1, GLA_HEADS):
            out = out + jnp.where((lane >= h * width) & (lane < (h + 1) * width), x[h * r:(h + 1) * r], 0.0)
        return out

    def chunk(ci):
        rows = pl.ds(ci * c, c)
        g = la_ref[rows, :]
        g_hi = g.astype(BF16)
        g_lo = (g - g_hi.astype(F32)).astype(BF16)
        cs = _dot(sel_ref[...], jnp.concatenate([g_hi, g_lo], axis=1))
        cs = cs[:, :GLA_QK] + cs[:, GLA_QK:]
        e_b = jnp.exp(cs[0:c])
        e_tail = jnp.exp(cs[c:2 * c])
        bw = cs[2 * c:3 * c]
        e_q = jnp.exp(bw)
        e_k = jnp.exp(cs[3 * c:4 * c])

        q = q_ref[rows, :].astype(F32) * (GLA_DK ** -0.5)
        k = k_ref[rows, :].astype(F32)
        v = v_ref[rows, :]
        vf = v.astype(F32)

        st = st_ref[...]
        o_inter = _dot_nt((q * e_b).astype(BF16), st.astype(BF16))
        upd = _dot_tn(v, (k * e_tail).astype(BF16))
        st_ref[...] = st * e_b[c - 1:c, :] + upd * bd_ref[...]

        qs = q * e_q
        ks = k * e_k
        atts = []
        for i in range(1, GLA_NSUB):
            ki = (ks * jnp.exp(cs[(3 + i) * c:(4 + i) * c])).astype(BF16)
            qst = head_rows(qs[i * r:(i + 1) * r], lane_qk, GLA_DK).astype(BF16)
            att = _dot_nt(qst, ki)
            atts.append(jnp.where(col_chunk < i * r, att, 0.0))
        ov = _dot(jnp.concatenate(atts, axis=0).astype(BF16), v)

        outs = []
        for i in range(GLA_NSUB):
            sub = slice(i * r, (i + 1) * r)
            qi, ki, bwi, vi = q[sub], k[sub], bw[sub], vf[sub]
            hr = r // 2
            upper = slice(i * r + hr, (i + 1) * r)
            q_up, bw_up = q[upper], bw[upper]
            ps = []
            for s in range(r):
                if s < hr:
                    e = jnp.exp(jnp.where(row_sub >= s, bwi - bwi[s:s + 1, :], NEG_BIG))
                    ps.append((qi * ki[s:s + 1, :]) * e)
                else:
                    e = jnp.exp(jnp.where(row_up >= s, bw_up - bwi[s:s + 1, :], NEG_BIG))
                    ps.append((q_up * ki[s:s + 1, :]) * e)
            rep = _dot(jnp.concatenate(ps, axis=0).astype(BF16), ex_ref[...])
            o_lo = o_inter[i * r:i * r + hr]
            o_hi = o_inter[i * r + hr:(i + 1) * r]
            for s in range(hr):
                o_lo = o_lo + rep[s * r:s * r + hr] * vi[s:s + 1, :]
                o_hi = o_hi + rep[s * r + hr:(s + 1) * r] * vi[s:s + 1, :]
            for s in range(hr, r):
                p0 = hr * r + (s - hr) * hr
                o_hi = o_hi + rep[p0:p0 + hr] * vi[s:s + 1, :]
            o_i = jnp.concatenate([o_lo, o_hi], axis=0)
            if i > 0:
                o_i = o_i + head_diag(ov[(i - 1) * GLA_HEADS * r:i * GLA_HEADS * r], lane_v, GLA_DV)
            outs.append(o_i)
        o = jnp.concatenate(outs, axis=0)

        normed = []
        for h in range(GLA_HEADS):
            oh = o[:, h * GLA_DV:(h + 1) * GLA_DV]
            normed.append(oh * lax.rsqrt(jnp.mean(oh * oh, axis=-1, keepdims=True) + EPS))
        y = jnp.concatenate(normed, axis=1) * gn_ref[...]
        o_ref[rows, 0:GLA_V] = (y * _silu(r_ref[rows, :].astype(F32))).astype(o_ref.dtype)
    for ci in range(tc // c):
        chunk(ci)


def _rope_tables(positions):
    inv = 1.0 / (ROPE_THETA ** (jnp.arange(0, SWA_HEAD_DIM, 2, dtype=F32) / SWA_HEAD_DIM))
    ang = positions.astype(F32)[:, None] * inv
    cos, sin = jnp.cos(ang), jnp.sin(ang)
    reps = LANES // SWA_HEAD_DIM
    return (jnp.tile(jnp.concatenate([cos, cos], axis=-1), (1, reps)),
            jnp.tile(jnp.concatenate([-sin, sin], axis=-1), (1, reps)))


def kernel(x, mem, positions, norm_mix, w_in, gla_w_gate, gla_b_gate, gla_norm, swa_sinks, sc_conv, w_out,
           norm_x, norm_mem, xa_wq, xa_wk, xa_wv, xa_wo, norm_ffn, ffn_w_up, ffn_conv, ffn_conv_b, ffn_w_down,
           norm_final):
    assert x.shape[0] == 1 and mem.shape[0] == 1
    depth = w_in.shape[0]
    h = x[0]
    m = mem[0]
    cos_t, sin_t = _rope_tables(positions[0])
    w_in_z = _reorder_w_in(jnp.swapaxes(w_in, 1, 2), 256)
    w_out_b = _cast_bf16(w_out, 512)
    w_gate = jnp.pad(gla_w_gate, ((0, 0), (0, LANES - GLA_GATE_RANK), (0, 0))).astype(BF16)
    mem_k, mem_v = _mem_kv(m, norm_mem, xa_wk, xa_wv, 512)

    for l in range(depth):
        h, hn = _mixer(h, norm_mix[l], w_in_z, w_out_b, l, cos_t, sin_t, swa_sinks[l], sc_conv[l],
                       w_gate[l], gla_b_gate[l], gla_norm[l], norm_x[l])
        o_x = _cross_attention(hn, xa_wq, l, mem_k, mem_v, 1024)
        h, hn = _matmul_resid_norm(o_x, xa_wo, l, h, norm_ffn[l], 512, "xa_out_proj")
        act = _ffn_up(hn, ffn_w_up, ffn_conv, ffn_conv_b, l, 1024, 512)
        h = _matmul_resid(act, ffn_w_down, l, h, 256, 1024, 1, "ffn_down")
    return _rmsnorm(h, norm_final, F32, 512)[None]
```

```python
import functools

import numpy as np
import jax
import jax.numpy as jnp
from jax import lax
from jax.experimental import pallas as pl
from jax.experimental.pallas import tpu as pltpu

F32 = jnp.float32
BF16 = jnp.bfloat16

D_MODEL = 2048
N_MEM = 256
XA_HEADS = 4
XA_HEAD_DIM = D_MODEL // XA_HEADS
GLA_HEADS = 4
GLA_DK = 64
GLA_DV = 128
GLA_GATE_RANK = 16
GLA_GATE_TAU = 16.0
SWA_Q_HEADS = 16
SWA_KV_HEADS = 2
SWA_HEAD_DIM = 64
SWA_WINDOW = 128
ROPE_THETA = 10000.0
SC_CH = 512
SC_WIDTH = 3
D_FF = 5632
EPS = 1e-6

GLA_QK = GLA_HEADS * GLA_DK
GLA_V = GLA_HEADS * GLA_DV
SWA_Q = SWA_Q_HEADS * SWA_HEAD_DIM
SWA_KV = SWA_KV_HEADS * SWA_HEAD_DIM
SWA_GROUP = SWA_Q // SWA_KV_HEADS
IN_SIZES = (GLA_QK, GLA_QK, GLA_V, GLA_V, GLA_GATE_RANK, SWA_Q, SWA_KV, SWA_KV, SC_CH, SC_CH, SC_CH)
N_IN = sum(IN_SIZES)

LANES = 128
F32_ROWS = 8
BF16_ROWS = 16
MXU_TILE = 256
VMEM_BYTES = 64 * 1024 * 1024
VMEM_LIMIT = VMEM_BYTES * 7 // 8

Z_Q, Z_K, Z_V, Z_R = 0, 256, 512, 1024
Z_SQ, Z_CB, Z_CC, Z_CH = 1536, 2560, 3072, 3584
Z_SK, Z_SV, Z_LR = 4096, 4224, 4352
Z_COLS = 4480

GLA_CHUNK = 64
GLA_SUB = 16
GLA_NSUB = GLA_CHUNK // GLA_SUB
NEG_BIG = -1e30
SWA_STEP = 2 * SWA_WINDOW


class _Tiles:
    rmsnorm_rows = 512
    relayout_cols = 256
    cast_rows = 512
    mem_kv_cols = 512
    in_proj_cols = 7 * MXU_TILE
    xa_rows = 1024
    xa_out_rows = 512
    ffn_up_rows = 1024
    ffn_up_cols = 512
    ffn_down_rows = 256
    ffn_down_cols = 1024


def _params(*sem):
    return pltpu.CompilerParams(dimension_semantics=sem, vmem_limit_bytes=VMEM_LIMIT)


def _dot(a, b):
    return jnp.dot(a, b, preferred_element_type=F32)


def _dot_nt(a, b):
    return lax.dot_general(a, b, (((1,), (1,)), ((), ())), preferred_element_type=F32)


def _dot_tn(a, b):
    return lax.dot_general(a, b, (((0,), (0,)), ((), ())), preferred_element_type=F32)


def _silu(x):
    return x / (1.0 + jnp.exp(-x))


def _normalize(x, gain):
    return x * lax.rsqrt(jnp.mean(x * x, axis=-1, keepdims=True) + EPS) * gain


def _causal_conv3(u, before, c):
    rows = u.shape[0]
    ext = jnp.concatenate([before[BF16_ROWS - F32_ROWS:], u], axis=0)
    return (c[2:3] * u + c[1:2] * ext[F32_ROWS - 1:F32_ROWS - 1 + rows]
            + c[0:1] * ext[F32_ROWS - 2:F32_ROWS - 2 + rows])


def _layer_cols(layer, k, tn, col_of):
    return pl.BlockSpec((None, k, tn), lambda *ids: (layer, 0, col_of(*ids)))


def _rmsnorm_body(x_ref, g_ref, o_ref):
    o_ref[...] = _normalize(x_ref[...], g_ref[...]).astype(o_ref.dtype)


def _rmsnorm(x, gain):
    m, d = x.shape
    tm = min(_Tiles.rmsnorm_rows, m)
    return pl.pallas_call(
        _rmsnorm_body,
        grid=(m // tm,),
        in_specs=[pl.BlockSpec((tm, d), lambda i: (i, 0)),
                  pl.BlockSpec((1, d), lambda i: (0, 0))],
        out_specs=pl.BlockSpec((tm, d), lambda i: (i, 0)),
        out_shape=jax.ShapeDtypeStruct((m, d), x.dtype),
        compiler_params=_params("parallel"),
        name="rmsnorm",
    )(x, gain.reshape(1, d))


def _reorder_body(wt_ref, o_ref):
    o = np.cumsum((0,) + IN_SIZES)
    wt = wt_ref[...]
    pad = jnp.zeros((LANES - GLA_GATE_RANK, wt.shape[1]), wt.dtype)
    zt = jnp.concatenate(
        [wt[o[0]:o[4]], wt[o[5]:o[6]], wt[o[8]:o[11]], wt[o[6]:o[8]], wt[o[4]:o[5]], pad], axis=0)
    o_ref[...] = zt.T.astype(o_ref.dtype)


def _reorder_w_in(w_in_t):
    depth, n, d = w_in_t.shape
    tk = _Tiles.relayout_cols
    return pl.pallas_call(
        _reorder_body,
        grid=(depth, d // tk),
        in_specs=[pl.BlockSpec((None, n, tk), lambda l, i: (l, 0, i))],
        out_specs=pl.BlockSpec((None, tk, Z_COLS), lambda l, i: (l, i, 0)),
        out_shape=jax.ShapeDtypeStruct((depth, d, Z_COLS), BF16),
        compiler_params=_params("parallel", "parallel"),
        name="reorder_w_in",
    )(w_in_t)


def _cast_body(w_ref, o_ref):
    o_ref[...] = w_ref[...].astype(o_ref.dtype)


def _cast_bf16(w):
    depth, k, n = w.shape
    tk = _Tiles.cast_rows
    spec = pl.BlockSpec((None, tk, n), lambda l, i: (l, i, 0))
    return pl.pallas_call(
        _cast_body,
        grid=(depth, k // tk),
        in_specs=[spec],
        out_specs=spec,
        out_shape=jax.ShapeDtypeStruct(w.shape, BF16),
        compiler_params=_params("parallel", "parallel"),
        name="cast_bf16",
    )(w)


def _mem_kv_body(m_ref, g_ref, wk_ref, wv_ref, k_ref, v_ref, mn_ref):
    @pl.when(pl.program_id(1) == 0)
    def _():
        mn_ref[...] = _normalize(m_ref[...], g_ref[...]).astype(BF16)

    mn = mn_ref[...]
    k_ref[...] = _dot(mn, wk_ref[...].astype(BF16)).astype(k_ref.dtype)
    v_ref[...] = _dot(mn, wv_ref[...].astype(BF16)).astype(v_ref.dtype)


def _mem_kv(mem, gains, wk, wv):
    n_mem, d = mem.shape
    depth = wk.shape[0]
    tn = _Tiles.mem_kv_cols
    w_spec = pl.BlockSpec((None, d, tn), lambda l, j: (l, 0, j))
    o_spec = pl.BlockSpec((None, n_mem, tn), lambda l, j: (l, 0, j))
    out = jax.ShapeDtypeStruct((depth, n_mem, d), BF16)
    return pl.pallas_call(
        _mem_kv_body,
        grid=(depth, d // tn),
        in_specs=[pl.BlockSpec((n_mem, d), lambda l, j: (0, 0)),
                  pl.BlockSpec((None, 1, d), lambda l, j: (l, 0, 0)),
                  w_spec, w_spec],
        out_specs=[o_spec, o_spec],
        out_shape=[out, out],
        scratch_shapes=[pltpu.VMEM((n_mem, d), BF16)],
        compiler_params=_params("parallel", "arbitrary"),
        name="mem_kv",
    )(mem, gains.reshape(depth, 1, d), wk, wv)


def _ffn_down_body(a_ref, w_ref, h_ref, o_ref, wb_ref):
    @pl.when(pl.program_id(1) == 0)
    def _():
        wb_ref[...] = w_ref[...].astype(BF16)

    o_ref[...] = h_ref[...] + _dot(a_ref[...], wb_ref[...])


def _ffn_down(a, w, layer, h):
    m, d = h.shape
    k = w.shape[1]
    tm, tn = min(_Tiles.ffn_down_rows, m), _Tiles.ffn_down_cols
    assert a.shape == (m, k)
    return pl.pallas_call(
        _ffn_down_body,
        grid=(d // tn, m // tm),
        in_specs=[pl.BlockSpec((tm, k), lambda j, i: (i, 0)),
                  pl.BlockSpec((None, k, tn), lambda j, i: (layer, 0, j), pipeline_mode=pl.Buffered(1)),
                  pl.BlockSpec((tm, tn), lambda j, i: (i, j))],
        out_specs=pl.BlockSpec((tm, tn), lambda j, i: (i, j)),
        out_shape=jax.ShapeDtypeStruct((m, d), F32),
        scratch_shapes=[pltpu.VMEM((k, tn), BF16)],
        compiler_params=_params("parallel", "arbitrary"),
        name="ffn_down",
    )(a, w, h)


def _xa_out_body(a_ref, w_ref, h_ref, g_ref, o_ref, hn_ref, wb_ref):
    @pl.when(pl.program_id(0) == 0)
    def _():
        wb_ref[...] = w_ref[...].astype(BF16)

    h = h_ref[...] + _dot(a_ref[...], wb_ref[...])
    o_ref[...] = h
    hn_ref[...] = _normalize(h, g_ref[...]).astype(hn_ref.dtype)


def _xa_out_proj(a, w, layer, h, gain):
    m, d = h.shape
    k = w.shape[1]
    tm = min(_Tiles.xa_out_rows, m)
    rows = pl.BlockSpec((tm, d), lambda i: (i, 0))
    return pl.pallas_call(
        _xa_out_body,
        grid=(m // tm,),
        in_specs=[pl.BlockSpec((tm, k), lambda i: (i, 0)),
                  pl.BlockSpec((None, k, d), lambda i: (layer, 0, 0), pipeline_mode=pl.Buffered(1)),
                  rows,
                  pl.BlockSpec((1, d), lambda i: (0, 0))],
        out_specs=[rows, rows],
        out_shape=[jax.ShapeDtypeStruct((m, d), F32), jax.ShapeDtypeStruct((m, d), BF16)],
        scratch_shapes=[pltpu.VMEM((k, d), BF16)],
        compiler_params=_params("arbitrary"),
        name="xa_out_proj",
    )(a, w, h, gain.reshape(1, d))


def _xa_body(hn_ref, wq_ref, k_ref, v_ref, o_ref):
    q = _dot(hn_ref[...], wq_ref[...].astype(BF16)).astype(BF16)
    s = _dot_nt(q, k_ref[...]) * (XA_HEAD_DIM ** -0.5)
    m = jnp.max(s, axis=-1, keepdims=True)
    p = jnp.exp(s - m)
    denom = jnp.sum(p, axis=-1, keepdims=True)
    o = _dot(p.astype(BF16), v_ref[...]) / denom
    o_ref[...] = o.astype(o_ref.dtype)


def _cross_attention(hn, wq, layer, mem_k, mem_v):
    m, d = hn.shape
    tm = min(_Tiles.xa_rows, m)
    hd = XA_HEAD_DIM
    return pl.pallas_call(
        _xa_body,
        grid=(m // tm, XA_HEADS),
        in_specs=[pl.BlockSpec((tm, d), lambda i, j: (i, 0)),
                  _layer_cols(layer, d, hd, lambda i, j: j),
                  _layer_cols(layer, N_MEM, hd, lambda i, j: j),
                  _layer_cols(layer, N_MEM, hd, lambda i, j: j)],
        out_specs=pl.BlockSpec((tm, hd), lambda i, j: (i, j)),
        out_shape=jax.ShapeDtypeStruct((m, d), BF16),
        compiler_params=_params("parallel", "arbitrary"),
        name="cross_attention",
    )(hn, wq, mem_k, mem_v)


def _ffn_up_body(hn_ref, halo_ref, wg_ref, wv_ref, cg_ref, cv_ref, bg_ref, bv_ref, o_ref):
    a = hn_ref[...]
    halo = jnp.where(pl.program_id(0) > 0, halo_ref[...], jnp.zeros_like(halo_ref))

    def conv_branch(w_ref, c_ref, b_ref):
        w = w_ref[...].astype(BF16)
        u = _dot(a, w)
        before = _dot(halo, w)
        return _causal_conv3(u, before, c_ref[...]) + b_ref[...]

    g = conv_branch(wg_ref, cg_ref, bg_ref)
    val = conv_branch(wv_ref, cv_ref, bv_ref)
    o_ref[...] = (_silu(g) * val).astype(o_ref.dtype)


def _ffn_up(hn, w_up, conv_w, conv_b, layer):
    m, d = hn.shape
    tm, tn = min(_Tiles.ffn_up_rows, m), _Tiles.ffn_up_cols
    nj = D_FF // tn
    halo_blocks = tm // BF16_ROWS

    def vec(rows, col_of):
        return pl.BlockSpec((None, rows, tn), lambda i, j: (layer, 0, col_of(j)))

    return pl.pallas_call(
        _ffn_up_body,
        grid=(m // tm, nj),
        in_specs=[pl.BlockSpec((tm, d), lambda i, j: (i, 0)),
                  pl.BlockSpec((BF16_ROWS, d), lambda i, j: (jnp.maximum(i * halo_blocks - 1, 0), 0)),
                  _layer_cols(layer, d, tn, lambda i, j: j),
                  _layer_cols(layer, d, tn, lambda i, j: nj + j),
                  vec(SC_WIDTH, lambda j: j), vec(SC_WIDTH, lambda j: nj + j),
                  vec(1, lambda j: j), vec(1, lambda j: nj + j)],
        out_specs=pl.BlockSpec((tm, tn), lambda i, j: (i, j)),
        out_shape=jax.ShapeDtypeStruct((m, D_FF), BF16),
        compiler_params=_params("parallel", "arbitrary"),
        name="ffn_up",
    )(hn, hn, w_up, w_up, conv_w, conv_w,
      conv_b.reshape(conv_b.shape[0], 1, -1), conv_b.reshape(conv_b.shape[0], 1, -1))


def _mixer_body(sink_ref, cosc_ref, sinc_ref, cosp_ref, sinp_ref, cw_ref,
                wg_ref, bg_ref, gn_ref, sel_ref, ex_ref, bd_ref,
                hin_ref, gin_ref, wz_ref, hres_ref, wo_ref, gx_ref, o_ref, hn_ref,
                znew_ref, zcur_ref, kvprev_ref, halo_ref, mix_ref, st_ref, la_ref, *, last_tile):
    w = SWA_WINDOW
    s = pl.program_id(0)

    @pl.when(s == 0)
    def _():
        znew_ref[...] = jnp.zeros_like(znew_ref)
        zcur_ref[...] = jnp.zeros_like(zcur_ref)
        mix_ref[...] = jnp.zeros_like(mix_ref)

    @pl.when(s <= 1)
    def _():
        st_ref[...] = jnp.zeros_like(st_ref)

    kvprev_ref[...] = zcur_ref[w:, Z_SK:Z_SK + 2 * SWA_KV]
    halo_ref[...] = zcur_ref[SWA_STEP - BF16_ROWS:, Z_CC:Z_CC + 2 * SC_CH]
    zcur_ref[...] = znew_ref[...]

    h1 = hres_ref[...] + _dot(mix_ref[...], wo_ref[...])
    o_ref[...] = h1
    hn_ref[...] = _normalize(h1, gx_ref[...]).astype(hn_ref.dtype)

    hn_in = _normalize(hin_ref[...], gin_ref[...]).astype(BF16)
    for c0 in range(0, Z_COLS, _Tiles.in_proj_cols):
        c1 = min(c0 + _Tiles.in_proj_cols, Z_COLS)
        znew_ref[:, c0:c1] = _dot(hn_in, wz_ref[:, c0:c1]).astype(BF16)

    def zcols(off, width):
        return zcur_ref.at[:, off:off + width]

    q0_ref, q1_ref = zcols(Z_SQ, SWA_GROUP), zcols(Z_SQ + SWA_GROUP, SWA_GROUP)
    kc_ref, vc_ref = zcols(Z_SK, SWA_KV), zcols(Z_SV, SWA_KV)
    kp_ref, vp_ref = kvprev_ref.at[:, 0:SWA_KV], kvprev_ref.at[:, SWA_KV:2 * SWA_KV]
    cb_ref, cc_ref, ch_ref = zcols(Z_CB, SC_CH), zcols(Z_CC, SC_CH), zcols(Z_CH, SC_CH)
    hcc_ref, hch_ref = halo_ref.at[:, 0:SC_CH], halo_ref.at[:, SC_CH:2 * SC_CH]

    _gla_tile(zcols(Z_Q, GLA_QK), zcols(Z_K, GLA_QK), zcols(Z_V, GLA_V), zcols(Z_R, GLA_V), zcols(Z_LR, LANES),
              wg_ref, bg_ref, gn_ref, sel_ref, ex_ref, bd_ref, st_ref, la_ref, mix_ref)

    n = jnp.clip(s - 1, 0, last_tile)
    lane = lax.broadcasted_iota(jnp.int32, (1, LANES), 1)
    low_half = lane < SWA_HEAD_DIM
    first_rot = (lane & (SWA_HEAD_DIM - 1)) < SWA_HEAD_DIM // 2

    def rope(x, cos, sin):
        half = SWA_HEAD_DIM // 2
        swapped = jnp.where(first_rot, pltpu.roll(x, LANES - half, 1), pltpu.roll(x, half, 1))
        return x * cos + swapped * sin

    cosc, sinc = cosc_ref[...], sinc_ref[...]
    k_cur = rope(kc_ref[...].astype(F32), cosc, sinc)
    k_prev = rope(kp_ref[...].astype(F32), cosp_ref[...], sinp_ref[...])
    v_cur = vc_ref[...].astype(F32)
    v_prev = vp_ref[...].astype(F32)

    qi = lax.broadcasted_iota(jnp.int32, (w, 2 * w), 0)
    ki = lax.broadcasted_iota(jnp.int32, (w, 2 * w), 1)
    in_window = (ki > qi) & (ki <= qi + w)
    row2 = lax.broadcasted_iota(jnp.int32, (2 * w, 1), 0)

    pairs = SWA_GROUP // LANES
    for b in range(SWA_STEP // w):
        rows = slice(b * w, (b + 1) * w)
        if b == 0:
            k_all = jnp.concatenate([k_prev, k_cur[rows]], axis=0)
            v_all = jnp.concatenate([v_prev, v_cur[rows]], axis=0)
            allowed = in_window & ((n > 0) | (ki >= w))
        else:
            k_all = k_cur[(b - 1) * w:(b + 1) * w]
            v_all = v_cur[(b - 1) * w:(b + 1) * w]
            allowed = in_window
        bias = jnp.where(allowed, 0.0, NEG_BIG)
        bias2 = jnp.concatenate([bias, bias], axis=0)
        k_rot = pltpu.roll(k_all, SWA_HEAD_DIM, 1)
        v_rot = pltpu.roll(v_all, SWA_HEAD_DIM, 1)
        for g, q_ref in enumerate((q0_ref, q1_ref)):
            if g == 0:
                kg = jnp.where(low_half, k_all, k_rot).astype(BF16)
                vg = jnp.where(low_half, v_all, v_rot).astype(BF16)
            else:
                kg = jnp.where(low_half, k_rot, k_all).astype(BF16)
                vg = jnp.where(low_half, v_rot, v_all).astype(BF16)
            for p in range(pairs):
                qp = rope(q_ref[rows, p * LANES:(p + 1) * LANES].astype(F32), cosc[rows], sinc[rows])
                qp = qp * (SWA_HEAD_DIM ** -0.5)
                stack = jnp.concatenate([jnp.where(low_half, qp, 0.0), jnp.where(low_half, 0.0, qp)], axis=0)
                s = _dot_nt(stack.astype(BF16), kg) + bias2
                head = g * (SWA_Q_HEADS // SWA_KV_HEADS) + 2 * p
                sink = jnp.where(row2 < w, sink_ref[head], sink_ref[head + 1])
                mx = jnp.maximum(jnp.max(s, axis=-1, keepdims=True), sink)
                pr = jnp.exp(s - mx)
                denom = jnp.sum(pr, axis=-1, keepdims=True) + jnp.exp(sink - mx)
                o = _dot(pr.astype(BF16), vg) / denom
                col = GLA_V + (g * pairs + p) * LANES
                mix_ref[rows, col:col + LANES] = jnp.where(low_half, o[:w], o[w:]).astype(mix_ref.dtype)

    u = cc_ref[...].astype(F32) * ch_ref[...].astype(F32)
    before = hcc_ref[...].astype(F32) * hch_ref[...].astype(F32)
    before = jnp.where(n > 0, before, 0.0)
    y = _causal_conv3(u, before, cw_ref[...])
    mix_ref[:, GLA_V + SWA_Q:] = (cb_ref[...].astype(F32) * y).astype(mix_ref.dtype)


def _mixer(h, in_gain, w_z, w_out, layer, cos_t, sin_t, sinks, conv_w, w_gate, b_gate, gla_gain, next_gain):
    t, d = h.shape
    w, st = SWA_WINDOW, SWA_STEP
    nt = t // st
    assert d == GLA_V + SWA_Q + SC_CH
    sel, expand, bd_mask = _gla_constants()
    gain_t = jnp.tile(gla_gain.reshape(1, GLA_DV), (1, GLA_HEADS))
    b_gate = b_gate.reshape(1, GLA_QK)

    def whole(a):
        return pl.BlockSpec(a.shape, lambda i: (0, 0))

    def mixed(i):
        return jnp.clip(i - 1, 0, nt - 1)

    def rows(width, tile_of):
        return pl.BlockSpec((st, width), lambda i: (tile_of(i), 0))

    def resident(k, n):
        return pl.BlockSpec((None, k, n), lambda i: (layer, 0, 0), pipeline_mode=pl.Buffered(1))

    def projected(i):
        return jnp.clip(i - 2, 0, nt - 1)

    tab_c = rows(LANES, mixed)
    tab_p = pl.BlockSpec((w, LANES), lambda i: (jnp.maximum(mixed(i) * (st // w) - 1, 0), 0))
    return pl.pallas_call(
        functools.partial(_mixer_body, last_tile=nt - 1),
        grid=(nt + 2,),
        in_specs=[pl.BlockSpec(memory_space=pltpu.SMEM),
                  tab_c, tab_c, tab_p, tab_p,
                  pl.BlockSpec((SC_WIDTH, SC_CH), lambda i: (0, 0)),
                  whole(w_gate), whole(b_gate), whole(gain_t), whole(sel), whole(expand), whole(bd_mask),
                  rows(d, lambda i: jnp.minimum(i, nt - 1)), whole(in_gain.reshape(1, d)), resident(d, Z_COLS),
                  rows(d, projected), resident(d, d), whole(next_gain.reshape(1, d))],
        out_specs=[rows(d, projected), rows(d, projected)],
        out_shape=[jax.ShapeDtypeStruct((t, d), F32), jax.ShapeDtypeStruct((t, d), BF16)],
        scratch_shapes=[pltpu.VMEM((st, Z_COLS), BF16), pltpu.VMEM((st, Z_COLS), BF16),
                        pltpu.VMEM((w, 2 * SWA_KV), BF16), pltpu.VMEM((BF16_ROWS, 2 * SC_CH), BF16),
                        pltpu.VMEM((st, d), BF16),
                        pltpu.VMEM((GLA_V, GLA_QK), F32), pltpu.VMEM((st, GLA_QK), F32)],
        compiler_params=_params("arbitrary"),
        name="mixer",
    )(sinks, cos_t, sin_t, cos_t, sin_t, conv_w, w_gate, b_gate, gain_t, sel, expand, bd_mask,
      h, in_gain.reshape(1, d), w_z, h, w_out, next_gain.reshape(1, d))


def _gla_constants():
    c, r = GLA_CHUNK, GLA_SUB
    t = np.arange(c)[:, None]
    s = np.arange(c)[None, :]
    blk_t, blk_s = t // r, s // r
    mats = [
        (s <= t),
        (s > t),
        (blk_s == blk_t) & (s <= t) & (s > blk_t * r),
        (blk_s == blk_t) & (s > t),
    ]
    for i in range(1, GLA_NSUB):
        mats.append((blk_t < i) & (s > blk_t * r + r - 1) & (s <= i * r))
    sel = np.concatenate(mats, axis=0).astype(np.float32)
    d_head = np.arange(GLA_QK)[:, None] // GLA_DK
    e_head = np.arange(GLA_V)[None, :] // GLA_DV
    expand = (d_head == e_head).astype(np.float32)
    return jnp.asarray(sel, BF16), jnp.asarray(expand, BF16), jnp.asarray(expand.T, F32)


def _gla_tile(q_ref, k_ref, v_ref, r_ref, lr_ref, wg_ref, bg_ref, gn_ref, sel_ref, ex_ref, bd_ref,
              st_ref, la_ref, o_ref):
    c, r = GLA_CHUNK, GLA_SUB
    tc = q_ref.shape[0]

    pre = _dot(lr_ref[...], wg_ref[...]) + bg_ref[...]
    la_ref[...] = (jnp.minimum(pre, 0.0) - jnp.log(1.0 + jnp.exp(-jnp.abs(pre)))) * (1.0 / GLA_GATE_TAU)

    lane_qk = lax.broadcasted_iota(jnp.int32, (r, GLA_QK), 1)
    lane_v = lax.broadcasted_iota(jnp.int32, (r, GLA_V), 1)
    row_sub = lax.broadcasted_iota(jnp.int32, (r, GLA_QK), 0)
    row_up = lax.broadcasted_iota(jnp.int32, (r // 2, GLA_QK), 0) + r // 2
    col_chunk = lax.broadcasted_iota(jnp.int32, (c, c), 1)

    def head_rows(x, lane, width):
        return jnp.concatenate(
            [jnp.where((lane >= h * width) & (lane < (h + 1) * width), x, 0.0) for h in range(GLA_HEADS)],
            axis=0)

    def head_diag(x, lane, width):
        out = jnp.where(lane < width, x[0:r], 0.0)
        for h in range(1, GLA_HEADS):
            out = out + jnp.where((lane >= h * width) & (lane < (h + 1) * width), x[h * r:(h + 1) * r], 0.0)
        return out

    def chunk(ci):
        rows = pl.ds(ci * c, c)
        g = la_ref[rows, :]
        g_hi = g.astype(BF16)
        g_lo = (g - g_hi.astype(F32)).astype(BF16)
        cs = _dot(sel_ref[...], jnp.concatenate([g_hi, g_lo], axis=1))
        cs = cs[:, :GLA_QK] + cs[:, GLA_QK:]
        e_b = jnp.exp(cs[0:c])
        e_tail = jnp.exp(cs[c:2 * c])
        bw = cs[2 * c:3 * c]
        e_q = jnp.exp(bw)
        e_k = jnp.exp(cs[3 * c:4 * c])

        q = q_ref[rows, :].astype(F32) * (GLA_DK ** -0.5)
        k = k_ref[rows, :].astype(F32)
        v = v_ref[rows, :]
        vf = v.astype(F32)

        st = st_ref[...]
        o_inter = _dot_nt((q * e_b).astype(BF16), st.astype(BF16))
        upd = _dot_tn(v, (k * e_tail).astype(BF16))
        st_ref[...] = st * e_b[c - 1:c, :] + upd * bd_ref[...]

        qs = q * e_q
        ks = k * e_k
        atts = []
        for i in range(1, GLA_NSUB):
            ki = (ks * jnp.exp(cs[(3 + i) * c:(4 + i) * c])).astype(BF16)
            qst = head_rows(qs[i * r:(i + 1) * r], lane_qk, GLA_DK).astype(BF16)
            att = _dot_nt(qst, ki)
            atts.append(jnp.where(col_chunk < i * r, att, 0.0))
        ov = _dot(jnp.concatenate(atts, axis=0).astype(BF16), v)

        outs = []
        for i in range(GLA_NSUB):
            sub = slice(i * r, (i + 1) * r)
            qi, ki, bwi, vi = q[sub], k[sub], bw[sub], vf[sub]
            hr = r // 2
            upper = slice(i * r + hr, (i + 1) * r)
            q_up, bw_up = q[upper], bw[upper]
            ps = []
            for s in range(r):
                if s < hr:
                    e = jnp.exp(jnp.where(row_sub >= s, bwi - bwi[s:s + 1, :], NEG_BIG))
                    ps.append((qi * ki[s:s + 1, :]) * e)
                else:
                    e = jnp.exp(jnp.where(row_up >= s, bw_up - bwi[s:s + 1, :], NEG_BIG))
                    ps.append((q_up * ki[s:s + 1, :]) * e)
            rep = _dot(jnp.concatenate(ps, axis=0).astype(BF16), ex_ref[...])
            o_lo = o_inter[i * r:i * r + hr]
            o_hi = o_inter[i * r + hr:(i + 1) * r]
            for s in range(hr):
                o_lo = o_lo + rep[s * r:s * r + hr] * vi[s:s + 1, :]
                o_hi = o_hi + rep[s * r + hr:(s + 1) * r] * vi[s:s + 1, :]
            for s in range(hr, r):
                p0 = hr * r + (s - hr) * hr
                o_hi = o_hi + rep[p0:p0 + hr] * vi[s:s + 1, :]
            o_i = jnp.concatenate([o_lo, o_hi], axis=0)
            if i > 0:
                o_i = o_i + head_diag(ov[(i - 1) * GLA_HEADS * r:i * GLA_HEADS * r], lane_v, GLA_DV)
            outs.append(o_i)
        o = jnp.concatenate(outs, axis=0)

        normed = []
        for h in range(GLA_HEADS):
            oh = o[:, h * GLA_DV:(h + 1) * GLA_DV]
            normed.append(oh * lax.rsqrt(jnp.mean(oh * oh, axis=-1, keepdims=True) + EPS))
        y = jnp.concatenate(normed, axis=1) * gn_ref[...]
        o_ref[rows, 0:GLA_V] = (y * _silu(r_ref[rows, :].astype(F32))).astype(o_ref.dtype)
    for ci in range(tc // c):
        chunk(ci)


def _rope_tables(positions):
    inv = 1.0 / (ROPE_THETA ** (jnp.arange(0, SWA_HEAD_DIM, 2, dtype=F32) / SWA_HEAD_DIM))
    ang = positions.astype(F32)[:, None] * inv
    cos, sin = jnp.cos(ang), jnp.sin(ang)
    reps = LANES // SWA_HEAD_DIM
    return (jnp.tile(jnp.concatenate([cos, cos], axis=-1), (1, reps)),
            jnp.tile(jnp.concatenate([-sin, sin], axis=-1), (1, reps)))


def kernel(x, mem, positions, norm_mix, w_in, gla_w_gate, gla_b_gate, gla_norm, swa_sinks, sc_conv, w_out,
           norm_x, norm_mem, xa_wq, xa_wk, xa_wv, xa_wo, norm_ffn, ffn_w_up, ffn_conv, ffn_conv_b, ffn_w_down,
           norm_final):
    assert x.shape[0] == 1 and mem.shape[0] == 1 and x.shape[2] == D_MODEL
    assert x.shape[1] % SWA_STEP == 0 and w_in.shape[2] == N_IN and sc_conv.shape[1] == ffn_conv.shape[1] == SC_WIDTH
    depth = w_in.shape[0]
    h = x[0]
    m = mem[0]
    cos_t, sin_t = _rope_tables(positions[0])
    w_in_z = _reorder_w_in(jnp.swapaxes(w_in, 1, 2))
    w_out_b = _cast_bf16(w_out)
    w_gate = jnp.pad(gla_w_gate, ((0, 0), (0, LANES - GLA_GATE_RANK), (0, 0))).astype(BF16)
    mem_k, mem_v = _mem_kv(m, norm_mem, xa_wk, xa_wv)

    for l in range(depth):
        h, hn = _mixer(h, norm_mix[l], w_in_z, w_out_b, l, cos_t, sin_t, swa_sinks[l], sc_conv[l],
                       w_gate[l], gla_b_gate[l], gla_norm[l], norm_x[l])
        o_x = _cross_attention(hn, xa_wq, l, mem_k, mem_v)
        h, hn = _xa_out_proj(o_x, xa_wo, l, h, norm_ffn[l])
        act = _ffn_up(hn, ffn_w_up, ffn_conv, ffn_conv_b, l)
        h = _ffn_down(act, ffn_w_down, l, h)
    return _rmsnorm(h, norm_final)[None]
```

```python
import functools

import numpy as np
import jax
import jax.numpy as jnp
from jax import lax
from jax.experimental import pallas as pl
from jax.experimental.pallas import tpu as pltpu

F32 = jnp.float32
BF16 = jnp.bfloat16

D_MODEL = 2048
N_MEM = 256
XA_HEADS = 4
XA_HEAD_DIM = D_MODEL // XA_HEADS
GLA_HEADS = 4
GLA_DK = 64
GLA_DV = 128
GLA_GATE_RANK = 16
GLA_GATE_TAU = 16.0
SWA_Q_HEADS = 16
SWA_KV_HEADS = 2
SWA_HEAD_DIM = 64
SWA_WINDOW = 128
ROPE_THETA = 10000.0
SC_CH = 512
SC_WIDTH = 3
D_FF = 5632
EPS = 1e-6

GLA_QK = GLA_HEADS * GLA_DK
GLA_V = GLA_HEADS * GLA_DV
SWA_Q = SWA_Q_HEADS * SWA_HEAD_DIM
SWA_KV = SWA_KV_HEADS * SWA_HEAD_DIM
SWA_GROUP = SWA_Q // SWA_KV_HEADS
IN_SIZES = (GLA_QK, GLA_QK, GLA_V, GLA_V, GLA_GATE_RANK, SWA_Q, SWA_KV, SWA_KV, SC_CH, SC_CH, SC_CH)
N_IN = sum(IN_SIZES)

LANES = 128
F32_ROWS = 8
BF16_ROWS = 16
MXU_TILE = 256
VMEM_BYTES = 64 * 1024 * 1024
VMEM_LIMIT = VMEM_BYTES * 7 // 8

Z_Q, Z_K, Z_V, Z_R = 0, 256, 512, 1024
Z_SQ, Z_CB, Z_CC, Z_CH = 1536, 2560, 3072, 3584
Z_SK, Z_SV, Z_LR = 4096, 4224, 4352
Z_COLS = 4480

GLA_CHUNK = 64
GLA_SUB = 16
GLA_NSUB = GLA_CHUNK // GLA_SUB
NEG_BIG = -1e30
SWA_STEP = 2 * SWA_WINDOW


class _Tiles:
    rmsnorm_rows = 512
    relayout_cols = 256
    cast_rows = 512
    mem_kv_cols = 512
    in_proj_cols = 7 * MXU_TILE
    xa_rows = 2048
    xa_out_rows = 512
    ffn_up_rows = 1024
    ffn_up_cols = 512
    ffn_down_rows = 512
    ffn_down_cols = 1024


def _params(*sem):
    return pltpu.CompilerParams(dimension_semantics=sem, vmem_limit_bytes=VMEM_LIMIT)


def _dot(a, b):
    return jnp.dot(a, b, preferred_element_type=F32)


def _dot_nt(a, b):
    return lax.dot_general(a, b, (((1,), (1,)), ((), ())), preferred_element_type=F32)


def _dot_tn(a, b):
    return lax.dot_general(a, b, (((0,), (0,)), ((), ())), preferred_element_type=F32)


def _silu(x):
    return x / (1.0 + jnp.exp(-x))


def _normalize(x, gain):
    return x * lax.rsqrt(jnp.mean(x * x, axis=-1, keepdims=True) + EPS) * gain


def _causal_conv3(u, before, c):
    rows = u.shape[0]
    ext = jnp.concatenate([before[BF16_ROWS - F32_ROWS:], u], axis=0)
    return (c[2:3] * u + c[1:2] * ext[F32_ROWS - 1:F32_ROWS - 1 + rows]
            + c[0:1] * ext[F32_ROWS - 2:F32_ROWS - 2 + rows])


def _layer_cols(layer, k, tn, col_of):
    return pl.BlockSpec((None, k, tn), lambda *ids: (layer, 0, col_of(*ids)))


def _rmsnorm_body(x_ref, g_ref, o_ref):
    o_ref[...] = _normalize(x_ref[...], g_ref[...]).astype(o_ref.dtype)


def _rmsnorm(x, gain):
    m, d = x.shape
    tm = min(_Tiles.rmsnorm_rows, m)
    return pl.pallas_call(
        _rmsnorm_body,
        grid=(m // tm,),
        in_specs=[pl.BlockSpec((tm, d), lambda i: (i, 0)),
                  pl.BlockSpec((1, d), lambda i: (0, 0))],
        out_specs=pl.BlockSpec((tm, d), lambda i: (i, 0)),
        out_shape=jax.ShapeDtypeStruct((m, d), x.dtype),
        compiler_params=_params("parallel"),
        name="rmsnorm",
    )(x, gain.reshape(1, d))


def _reorder_body(wt_ref, o_ref):
    o = np.cumsum((0,) + IN_SIZES)
    wt = wt_ref[...]
    pad = jnp.zeros((LANES - GLA_GATE_RANK, wt.shape[1]), wt.dtype)
    zt = jnp.concatenate(
        [wt[o[0]:o[4]], wt[o[5]:o[6]], wt[o[8]:o[11]], wt[o[6]:o[8]], wt[o[4]:o[5]], pad], axis=0)
    o_ref[...] = zt.T.astype(o_ref.dtype)


def _reorder_w_in(w_in_t):
    depth, n, d = w_in_t.shape
    tk = _Tiles.relayout_cols
    return pl.pallas_call(
        _reorder_body,
        grid=(depth, d // tk),
        in_specs=[pl.BlockSpec((None, n, tk), lambda l, i: (l, 0, i))],
        out_specs=pl.BlockSpec((None, tk, Z_COLS), lambda l, i: (l, i, 0)),
        out_shape=jax.ShapeDtypeStruct((depth, d, Z_COLS), BF16),
        compiler_params=_params("parallel", "parallel"),
        name="reorder_w_in",
    )(w_in_t)


def _cast_body(w_ref, o_ref):
    o_ref[...] = w_ref[...].astype(o_ref.dtype)


def _cast_bf16(w):
    depth, k, n = w.shape
    tk = _Tiles.cast_rows
    spec = pl.BlockSpec((None, tk, n), lambda l, i: (l, i, 0))
    return pl.pallas_call(
        _cast_body,
        grid=(depth, k // tk),
        in_specs=[spec],
        out_specs=spec,
        out_shape=jax.ShapeDtypeStruct(w.shape, BF16),
        compiler_params=_params("parallel", "parallel"),
        name="cast_bf16",
    )(w)


def _mem_kv_body(m_ref, g_ref, wk_ref, wv_ref, k_ref, v_ref, mn_ref):
    @pl.when(pl.program_id(1) == 0)
    def _():
        mn_ref[...] = _normalize(m_ref[...], g_ref[...]).astype(BF16)

    mn = mn_ref[...]
    k_ref[...] = _dot(mn, wk_ref[...].astype(BF16)).astype(k_ref.dtype)
    v_ref[...] = _dot(mn, wv_ref[...].astype(BF16)).astype(v_ref.dtype)


def _mem_kv(mem, gains, wk, wv):
    n_mem, d = mem.shape
    depth = wk.shape[0]
    tn = _Tiles.mem_kv_cols
    w_spec = pl.BlockSpec((None, d, tn), lambda l, j: (l, 0, j))
    o_spec = pl.BlockSpec((None, n_mem, tn), lambda l, j: (l, 0, j))
    out = jax.ShapeDtypeStruct((depth, n_mem, d), BF16)
    return pl.pallas_call(
        _mem_kv_body,
        grid=(depth, d // tn),
        in_specs=[pl.BlockSpec((n_mem, d), lambda l, j: (0, 0)),
                  pl.BlockSpec((None, 1, d), lambda l, j: (l, 0, 0)),
                  w_spec, w_spec],
        out_specs=[o_spec, o_spec],
        out_shape=[out, out],
        scratch_shapes=[pltpu.VMEM((n_mem, d), BF16)],
        compiler_params=_params("parallel", "arbitrary"),
        name="mem_kv",
    )(mem, gains.reshape(depth, 1, d), wk, wv)


def _ffn_down_body(a_ref, w_ref, h_ref, o_ref, wb_ref):
    @pl.when(pl.program_id(1) == 0)
    def _():
        wb_ref[...] = w_ref[...].astype(BF16)

    o_ref[...] = h_ref[...] + _dot(a_ref[...], wb_ref[...])


def _ffn_down(a, w, layer, h):
    m, d = h.shape
    k = w.shape[1]
    tm, tn = min(_Tiles.ffn_down_rows, m), _Tiles.ffn_down_cols
    assert a.shape == (m, k)
    return pl.pallas_call(
        _ffn_down_body,
        grid=(d // tn, m // tm),
        in_specs=[pl.BlockSpec((tm, k), lambda j, i: (i, 0)),
                  pl.BlockSpec((None, k, tn), lambda j, i: (layer, 0, j), pipeline_mode=pl.Buffered(1)),
                  pl.BlockSpec((tm, tn), lambda j, i: (i, j))],
        out_specs=pl.BlockSpec((tm, tn), lambda j, i: (i, j)),
        out_shape=jax.ShapeDtypeStruct((m, d), F32),
        scratch_shapes=[pltpu.VMEM((k, tn), BF16)],
        compiler_params=_params("parallel", "arbitrary"),
        name="ffn_down",
    )(a, w, h)


def _xa_out_body(a_ref, w_ref, h_ref, g_ref, o_ref, hn_ref, wb_ref):
    @pl.when(pl.program_id(0) == 0)
    def _():
        wb_ref[...] = w_ref[...].astype(BF16)

    h = h_ref[...] + _dot(a_ref[...], wb_ref[...])
    o_ref[...] = h
    hn_ref[...] = _normalize(h, g_ref[...]).astype(hn_ref.dtype)


def _xa_out_proj(a, w, layer, h, gain):
    m, d = h.shape
    k = w.shape[1]
    tm = min(_Tiles.xa_out_rows, m)
    rows = pl.BlockSpec((tm, d), lambda i: (i, 0))
    return pl.pallas_call(
        _xa_out_body,
        grid=(m // tm,),
        in_specs=[pl.BlockSpec((tm, k), lambda i: (i, 0)),
                  pl.BlockSpec((None, k, d), lambda i: (layer, 0, 0), pipeline_mode=pl.Buffered(1)),
                  rows,
                  pl.BlockSpec((1, d), lambda i: (0, 0))],
        out_specs=[rows, rows],
        out_shape=[jax.ShapeDtypeStruct((m, d), F32), jax.ShapeDtypeStruct((m, d), BF16)],
        scratch_shapes=[pltpu.VMEM((k, d), BF16)],
        compiler_params=_params("arbitrary"),
        name="xa_out_proj",
    )(a, w, h, gain.reshape(1, d))


def _xa_body(hn_ref, wq_ref, k_ref, v_ref, o_ref):
    q = _dot(hn_ref[...], wq_ref[...].astype(BF16)).astype(BF16)
    s = _dot_nt(q, k_ref[...]) * (XA_HEAD_DIM ** -0.5)
    m = jnp.max(s, axis=-1, keepdims=True)
    p = jnp.exp(s - m)
    denom = jnp.sum(p, axis=-1, keepdims=True)
    o = _dot(p.astype(BF16), v_ref[...]) / denom
    o_ref[...] = o.astype(o_ref.dtype)


def _cross_attention(hn, wq, layer, mem_k, mem_v):
    m, d = hn.shape
    tm = min(_Tiles.xa_rows, m)
    hd = XA_HEAD_DIM
    return pl.pallas_call(
        _xa_body,
        grid=(m // tm, XA_HEADS),
        in_specs=[pl.BlockSpec((tm, d), lambda i, j: (i, 0)),
                  _layer_cols(layer, d, hd, lambda i, j: j),
                  _layer_cols(layer, N_MEM, hd, lambda i, j: j),
                  _layer_cols(layer, N_MEM, hd, lambda i, j: j)],
        out_specs=pl.BlockSpec((tm, hd), lambda i, j: (i, j)),
        out_shape=jax.ShapeDtypeStruct((m, d), BF16),
        compiler_params=_params("parallel", "arbitrary"),
        name="cross_attention",
    )(hn, wq, mem_k, mem_v)


def _ffn_up_body(hn_ref, halo_ref, wg_ref, wv_ref, cg_ref, cv_ref, bg_ref, bv_ref, o_ref):
    a = hn_ref[...]
    halo = jnp.where(pl.program_id(0) > 0, halo_ref[...], jnp.zeros_like(halo_ref))

    def conv_branch(w_ref, c_ref, b_ref):
        w = w_ref[...].astype(BF16)
        u = _dot(a, w)
        before = _dot(halo, w)
        return _causal_conv3(u, before, c_ref[...]) + b_ref[...]

    g = conv_branch(wg_ref, cg_ref, bg_ref)
    val = conv_branch(wv_ref, cv_ref, bv_ref)
    o_ref[...] = (_silu(g) * val).astype(o_ref.dtype)


def _ffn_up(hn, w_up, conv_w, conv_b, layer):
    m, d = hn.shape
    tm, tn = min(_Tiles.ffn_up_rows, m), _Tiles.ffn_up_cols
    nj = D_FF // tn
    halo_blocks = tm // BF16_ROWS

    def vec(rows, col_of):
        return pl.BlockSpec((None, rows, tn), lambda i, j: (layer, 0, col_of(j)))

    return pl.pallas_call(
        _ffn_up_body,
        grid=(m // tm, nj),
        in_specs=[pl.BlockSpec((tm, d), lambda i, j: (i, 0)),
                  pl.BlockSpec((BF16_ROWS, d), lambda i, j: (jnp.maximum(i * halo_blocks - 1, 0), 0)),
                  _layer_cols(layer, d, tn, lambda i, j: j),
                  _layer_cols(layer, d, tn, lambda i, j: nj + j),
                  vec(SC_WIDTH, lambda j: j), vec(SC_WIDTH, lambda j: nj + j),
                  vec(1, lambda j: j), vec(1, lambda j: nj + j)],
        out_specs=pl.BlockSpec((tm, tn), lambda i, j: (i, j)),
        out_shape=jax.ShapeDtypeStruct((m, D_FF), BF16),
        compiler_params=_params("parallel", "arbitrary"),
        name="ffn_up",
    )(hn, hn, w_up, w_up, conv_w, conv_w,
      conv_b.reshape(conv_b.shape[0], 1, -1), conv_b.reshape(conv_b.shape[0], 1, -1))


def _mixer_body(sink_ref, cosc_ref, sinc_ref, cosp_ref, sinp_ref, cw_ref,
                wg_ref, bg_ref, gn_ref, sel_ref, ex_ref, bd_ref,
                hin_ref, gin_ref, wz_ref, hres_ref, wo_ref, gx_ref, o_ref, hn_ref,
                znew_ref, zcur_ref, kvprev_ref, halo_ref, mix_ref, st_ref, la_ref, *, last_tile):
    w = SWA_WINDOW
    s = pl.program_id(0)

    @pl.when(s == 0)
    def _():
        znew_ref[...] = jnp.zeros_like(znew_ref)
        zcur_ref[...] = jnp.zeros_like(zcur_ref)
        mix_ref[...] = jnp.zeros_like(mix_ref)

    @pl.when(s <= 1)
    def _():
        st_ref[...] = jnp.zeros_like(st_ref)

    kvprev_ref[...] = zcur_ref[w:, Z_SK:Z_SK + 2 * SWA_KV]
    halo_ref[...] = zcur_ref[SWA_STEP - BF16_ROWS:, Z_CC:Z_CC + 2 * SC_CH]
    zcur_ref[...] = znew_ref[...]

    h1 = hres_ref[...] + _dot(mix_ref[...], wo_ref[...])
    o_ref[...] = h1
    hn_ref[...] = _normalize(h1, gx_ref[...]).astype(hn_ref.dtype)

    hn_in = _normalize(hin_ref[...], gin_ref[...]).astype(BF16)
    for c0 in range(0, Z_COLS, _Tiles.in_proj_cols):
        c1 = min(c0 + _Tiles.in_proj_cols, Z_COLS)
        znew_ref[:, c0:c1] = _dot(hn_in, wz_ref[:, c0:c1]).astype(BF16)

    def zcols(off, width):
        return zcur_ref.at[:, off:off + width]

    q0_ref, q1_ref = zcols(Z_SQ, SWA_GROUP), zcols(Z_SQ + SWA_GROUP, SWA_GROUP)
    kc_ref, vc_ref = zcols(Z_SK, SWA_KV), zcols(Z_SV, SWA_KV)
    kp_ref, vp_ref = kvprev_ref.at[:, 0:SWA_KV], kvprev_ref.at[:, SWA_KV:2 * SWA_KV]
    cb_ref, cc_ref, ch_ref = zcols(Z_CB, SC_CH), zcols(Z_CC, SC_CH), zcols(Z_CH, SC_CH)
    hcc_ref, hch_ref = halo_ref.at[:, 0:SC_CH], halo_ref.at[:, SC_CH:2 * SC_CH]

    _gla_tile(zcols(Z_Q, GLA_QK), zcols(Z_K, GLA_QK), zcols(Z_V, GLA_V), zcols(Z_R, GLA_V), zcols(Z_LR, LANES),
              wg_ref, bg_ref, gn_ref, sel_ref, ex_ref, bd_ref, st_ref, la_ref, mix_ref)

    n = jnp.clip(s - 1, 0, last_tile)
    lane = lax.broadcasted_iota(jnp.int32, (1, LANES), 1)
    low_half = lane < SWA_HEAD_DIM
    first_rot = (lane & (SWA_HEAD_DIM - 1)) < SWA_HEAD_DIM // 2

    def rope(x, cos, sin):
        half = SWA_HEAD_DIM // 2
        swapped = jnp.where(first_rot, pltpu.roll(x, LANES - half, 1), pltpu.roll(x, half, 1))
        return x * cos + swapped * sin

    cosc, sinc = cosc_ref[...], sinc_ref[...]
    k_cur = rope(kc_ref[...].astype(F32), cosc, sinc)
    k_prev = rope(kp_ref[...].astype(F32), cosp_ref[...], sinp_ref[...])
    v_cur = vc_ref[...].astype(F32)
    v_prev = vp_ref[...].astype(F32)

    qi = lax.broadcasted_iota(jnp.int32, (w, 2 * w), 0)
    ki = lax.broadcasted_iota(jnp.int32, (w, 2 * w), 1)
    in_window = (ki > qi) & (ki <= qi + w)
    row2 = lax.broadcasted_iota(jnp.int32, (2 * w, 1), 0)

    pairs = SWA_GROUP // LANES
    for b in range(SWA_STEP // w):
        rows = slice(b * w, (b + 1) * w)
        if b == 0:
            k_all = jnp.concatenate([k_prev, k_cur[rows]], axis=0)
            v_all = jnp.concatenate([v_prev, v_cur[rows]], axis=0)
            allowed = in_window & ((n > 0) | (ki >= w))
        else:
            k_all = k_cur[(b - 1) * w:(b + 1) * w]
            v_all = v_cur[(b - 1) * w:(b + 1) * w]
            allowed = in_window
        bias = jnp.where(allowed, 0.0, NEG_BIG)
        bias2 = jnp.concatenate([bias, bias], axis=0)
        k_rot = pltpu.roll(k_all, SWA_HEAD_DIM, 1)
        v_rot = pltpu.roll(v_all, SWA_HEAD_DIM, 1)
        for g, q_ref in enumerate((q0_ref, q1_ref)):
            if g == 0:
                kg = jnp.where(low_half, k_all, k_rot).astype(BF16)
                vg = jnp.where(low_half, v_all, v_rot).astype(BF16)
            else:
                kg = jnp.where(low_half, k_rot, k_all).astype(BF16)
                vg = jnp.where(low_half, v_rot, v_all).astype(BF16)
            for p in range(pairs):
                qp = rope(q_ref[rows, p * LANES:(p + 1) * LANES].astype(F32), cosc[rows], sinc[rows])
                qp = qp * (SWA_HEAD_DIM ** -0.5)
                stack = jnp.concatenate([jnp.where(low_half, qp, 0.0), jnp.where(low_half, 0.0, qp)], axis=0)
                s = _dot_nt(stack.astype(BF16), kg) + bias2
                head = g * (SWA_Q_HEADS // SWA_KV_HEADS) + 2 * p
                sink = jnp.where(row2 < w, sink_ref[head], sink_ref[head + 1])
                mx = jnp.maximum(jnp.max(s, axis=-1, keepdims=True), sink)
                pr = jnp.exp(s - mx)
                denom = jnp.sum(pr, axis=-1, keepdims=True) + jnp.exp(sink - mx)
                o = _dot(pr.astype(BF16), vg) / denom
                col = GLA_V + (g * pairs + p) * LANES
                mix_ref[rows, col:col + LANES] = jnp.where(low_half, o[:w], o[w:]).astype(mix_ref.dtype)

    u = cc_ref[...].astype(F32) * ch_ref[...].astype(F32)
    before = hcc_ref[...].astype(F32) * hch_ref[...].astype(F32)
    before = jnp.where(n > 0, before, 0.0)
    y = _causal_conv3(u, before, cw_ref[...])
    mix_ref[:, GLA_V + SWA_Q:] = (cb_ref[...].astype(F32) * y).astype(mix_ref.dtype)


def _mixer(h, in_gain, w_z, w_out, layer, cos_t, sin_t, sinks, conv_w, w_gate, b_gate, gla_gain, next_gain):
    t, d = h.shape
    w, st = SWA_WINDOW, SWA_STEP
    nt = t // st
    assert d == GLA_V + SWA_Q + SC_CH
    sel, expand, bd_mask = _gla_constants()
    gain_t = jnp.tile(gla_gain.reshape(1, GLA_DV), (1, GLA_HEADS))
    b_gate = b_gate.reshape(1, GLA_QK)

    def whole(a):
        return pl.BlockSpec(a.shape, lambda i: (0, 0))

    def mixed(i):
        return jnp.clip(i - 1, 0, nt - 1)

    def rows(width, tile_of):
        return pl.BlockSpec((st, width), lambda i: (tile_of(i), 0))

    def resident(k, n):
        return pl.BlockSpec((None, k, n), lambda i: (layer, 0, 0), pipeline_mode=pl.Buffered(1))

    def projected(i):
        return jnp.clip(i - 2, 0, nt - 1)

    tab_c = rows(LANES, mixed)
    tab_p = pl.BlockSpec((w, LANES), lambda i: (jnp.maximum(mixed(i) * (st // w) - 1, 0), 0))
    return pl.pallas_call(
        functools.partial(_mixer_body, last_tile=nt - 1),
        grid=(nt + 2,),
        in_specs=[pl.BlockSpec(memory_space=pltpu.SMEM),
                  tab_c, tab_c, tab_p, tab_p,
                  pl.BlockSpec((SC_WIDTH, SC_CH), lambda i: (0, 0)),
                  whole(w_gate), whole(b_gate), whole(gain_t), whole(sel), whole(expand), whole(bd_mask),
                  rows(d, lambda i: jnp.minimum(i, nt - 1)), whole(in_gain.reshape(1, d)), resident(d, Z_COLS),
                  rows(d, projected), resident(d, d), whole(next_gain.reshape(1, d))],
        out_specs=[rows(d, projected), rows(d, projected)],
        out_shape=[jax.ShapeDtypeStruct((t, d), F32), jax.ShapeDtypeStruct((t, d), BF16)],
        scratch_shapes=[pltpu.VMEM((st, Z_COLS), BF16), pltpu.VMEM((st, Z_COLS), BF16),
                        pltpu.VMEM((w, 2 * SWA_KV), BF16), pltpu.VMEM((BF16_ROWS, 2 * SC_CH), BF16),
                        pltpu.VMEM((st, d), BF16),
                        pltpu.VMEM((GLA_V, GLA_QK), F32), pltpu.VMEM((st, GLA_QK), F32)],
        compiler_params=_params("arbitrary"),
        name="mixer",
    )(sinks, cos_t, sin_t, cos_t, sin_t, conv_w, w_gate, b_gate, gain_t, sel, expand, bd_mask,
      h, in_gain.reshape(1, d), w_z, h, w_out, next_gain.reshape(1, d))


def _gla_constants():
    c, r = GLA_CHUNK, GLA_SUB
    t = np.arange(c)[:, None]
    s = np.arange(c)[None, :]
    blk_t, blk_s = t // r, s // r
    mats = [
        (s <= t),
        (s > t),
        (blk_s == blk_t) & (s <= t) & (s > blk_t * r),
        (blk_s == blk_t) & (s > t),
    ]
    for i in range(1, GLA_NSUB):
        mats.append((blk_t < i) & (s > blk_t * r + r - 1) & (s <= i * r))
    sel = np.concatenate(mats, axis=0).astype(np.float32)
    d_head = np.arange(GLA_QK)[:, None] // GLA_DK
    e_head = np.arange(GLA_V)[None, :] // GLA_DV
    expand = (d_head == e_head).astype(np.float32)
    return jnp.asarray(sel, BF16), jnp.asarray(expand, BF16), jnp.asarray(expand.T, F32)


def _gla_tile(q_ref, k_ref, v_ref, r_ref, lr_ref, wg_ref, bg_ref, gn_ref, sel_ref, ex_ref, bd_ref,
              st_ref, la_ref, o_ref):
    c, r = GLA_CHUNK, GLA_SUB
    tc = q_ref.shape[0]

    pre = _dot(lr_ref[...], wg_ref[...]) + bg_ref[...]
    la_ref[...] = (jnp.minimum(pre, 0.0) - jnp.log(1.0 + jnp.exp(-jnp.abs(pre)))) * (1.0 / GLA_GATE_TAU)

    lane_qk = lax.broadcasted_iota(jnp.int32, (r, GLA_QK), 1)
    lane_v = lax.broadcasted_iota(jnp.int32, (r, GLA_V), 1)
    row_sub = lax.broadcasted_iota(jnp.int32, (r, GLA_QK), 0)
    row_up = lax.broadcasted_iota(jnp.int32, (r // 2, GLA_QK), 0) + r // 2
    col_chunk = lax.broadcasted_iota(jnp.int32, (c, c), 1)

    def head_rows(x, lane, width):
        return jnp.concatenate(
            [jnp.where((lane >= h * width) & (lane < (h + 1) * width), x, 0.0) for h in range(GLA_HEADS)],
            axis=0)

    def head_diag(x, lane, width):
        out = jnp.where(lane < width, x[0:r], 0.0)
        for h in range(1, GLA_HEADS):
            out = out + jnp.where((lane >= h * width) & (lane < (h + 1) * width), x[h * r:(h + 1) * r], 0.0)
        return out

    def chunk(ci):
        rows = pl.ds(ci * c, c)
        g = la_ref[rows, :]
        g_hi = g.astype(BF16)
        g_lo = (g - g_hi.astype(F32)).astype(BF16)
        cs = _dot(sel_ref[...], jnp.concatenate([g_hi, g_lo], axis=1))
        cs = cs[:, :GLA_QK] + cs[:, GLA_QK:]
        e_b = jnp.exp(cs[0:c])
        e_tail = jnp.exp(cs[c:2 * c])
        bw = cs[2 * c:3 * c]
        e_q = jnp.exp(bw)
        e_k = jnp.exp(cs[3 * c:4 * c])

        q = q_ref[rows, :].astype(F32) * (GLA_DK ** -0.5)
        k = k_ref[rows, :].astype(F32)
        v = v_ref[rows, :]
        vf = v.astype(F32)

        st = st_ref[...]
        o_inter = _dot_nt((q * e_b).astype(BF16), st.astype(BF16))
        upd = _dot_tn(v, (k * e_tail).astype(BF16))
        st_ref[...] = st * e_b[c - 1:c, :] + upd * bd_ref[...]

        qs = q * e_q
        ks = k * e_k
        atts = []
        for i in range(1, GLA_NSUB):
            ki = (ks * jnp.exp(cs[(3 + i) * c:(4 + i) * c])).astype(BF16)
            qst = head_rows(qs[i * r:(i + 1) * r], lane_qk, GLA_DK).astype(BF16)
            att = _dot_nt(qst, ki)
            atts.append(jnp.where(col_chunk < i * r, att, 0.0))
        ov = _dot(jnp.concatenate(atts, axis=0).astype(BF16), v)

        outs = []
        for i in range(GLA_NSUB):
            sub = slice(i * r, (i + 1) * r)
            qi, ki, bwi, vi = q[sub], k[sub], bw[sub], vf[sub]
            hr = r // 2
            upper = slice(i * r + hr, (i + 1) * r)
            q_up, bw_up = q[upper], bw[upper]
            ps = []
            for s in range(r):
                if s < hr:
                    e = jnp.exp(jnp.where(row_sub >= s, bwi - bwi[s:s + 1, :], NEG_BIG))
                    ps.append((qi * ki[s:s + 1, :]) * e)
                else:
                    e = jnp.exp(jnp.where(row_up >= s, bw_up - bwi[s:s + 1, :], NEG_BIG))
                    ps.append((q_up * ki[s:s + 1, :]) * e)
            rep = _dot(jnp.concatenate(ps, axis=0).astype(BF16), ex_ref[...])
            o_lo = o_inter[i * r:i * r + hr]
            o_hi = o_inter[i * r + hr:(i + 1) * r]
            for s in range(hr):
                o_lo = o_lo + rep[s * r:s * r + hr] * vi[s:s + 1, :]
                o_hi = o_hi + rep[s * r + hr:(s + 1) * r] * vi[s:s + 1, :]
            for s in range(hr, r):
                p0 = hr * r + (s - hr) * hr
                o_hi = o_hi + rep[p0:p0 + hr] * vi[s:s + 1, :]
            o_i = jnp.concatenate([o_lo, o_hi], axis=0)
            if i > 0:
                o_i = o_i + head_diag(ov[(i - 1) * GLA_HEADS * r:i * GLA_HEADS * r], lane_v, GLA_DV)
            outs.append(o_i)
        o = jnp.concatenate(outs, axis=0)

        normed = []
        for h in range(GLA_HEADS):
            oh = o[:, h * GLA_DV:(h + 1) * GLA_DV]
            normed.append(oh * lax.rsqrt(jnp.mean(oh * oh, axis=-1, keepdims=True) + EPS))
        y = jnp.concatenate(normed, axis=1) * gn_ref[...]
        o_ref[rows, 0:GLA_V] = (y * _silu(r_ref[rows, :].astype(F32))).astype(o_ref.dtype)
    for ci in range(tc // c):
        chunk(ci)


def _rope_tables(positions):
    inv = 1.0 / (ROPE_THETA ** (jnp.arange(0, SWA_HEAD_DIM, 2, dtype=F32) / SWA_HEAD_DIM))
    ang = positions.astype(F32)[:, None] * inv
    cos, sin = jnp.cos(ang), jnp.sin(ang)
    reps = LANES // SWA_HEAD_DIM
    return (jnp.tile(jnp.concatenate([cos, cos], axis=-1), (1, reps)),
            jnp.tile(jnp.concatenate([-sin, sin], axis=-1), (1, reps)))


def kernel(x, mem, positions, norm_mix, w_in, gla_w_gate, gla_b_gate, gla_norm, swa_sinks, sc_conv, w_out,
           norm_x, norm_mem, xa_wq, xa_wk, xa_wv, xa_wo, norm_ffn, ffn_w_up, ffn_conv, ffn_conv_b, ffn_w_down,
           norm_final):
    assert x.shape[0] == 1 and mem.shape[0] == 1 and x.shape[2] == D_MODEL
    assert x.shape[1] % SWA_STEP == 0 and w_in.shape[2] == N_IN and sc_conv.shape[1] == ffn_conv.shape[1] == SC_WIDTH
    depth = w_in.shape[0]
    h = x[0]
    m = mem[0]
    cos_t, sin_t = _rope_tables(positions[0])
    w_in_z = _reorder_w_in(jnp.swapaxes(w_in, 1, 2))
    w_out_b = _cast_bf16(w_out)
    w_gate = jnp.pad(gla_w_gate, ((0, 0), (0, LANES - GLA_GATE_RANK), (0, 0))).astype(BF16)
    mem_k, mem_v = _mem_kv(m, norm_mem, xa_wk, xa_wv)

    for l in range(depth):
        h, hn = _mixer(h, norm_mix[l], w_in_z, w_out_b, l, cos_t, sin_t, swa_sinks[l], sc_conv[l],
                       w_gate[l], gla_b_gate[l], gla_norm[l], norm_x[l])
        o_x = _cross_attention(hn, xa_wq, l, mem_k, mem_v)
        h, hn = _xa_out_proj(o_x, xa_wo, l, h, norm_ffn[l])
        act = _ffn_up(hn, ffn_w_up, ffn_conv, ffn_conv_b, l)
        h = _ffn_down(act, ffn_w_down, l, h)
    return _rmsnorm(h, norm_final)[None]
```

```python
import functools

import numpy as np
import jax
import jax.numpy as jnp
from jax import lax
from jax.experimental import pallas as pl
from jax.experimental.pallas import tpu as pltpu

F32 = jnp.float32
BF16 = jnp.bfloat16

D_MODEL = 2048
N_MEM = 256
XA_HEADS = 4
XA_HEAD_DIM = D_MODEL // XA_HEADS
GLA_HEADS = 4
GLA_DK = 64
GLA_DV = 128
GLA_GATE_RANK = 16
GLA_GATE_TAU = 16.0
SWA_Q_HEADS = 16
SWA_KV_HEADS = 2
SWA_HEAD_DIM = 64
SWA_WINDOW = 128
ROPE_THETA = 10000.0
SC_CH = 512
SC_WIDTH = 3
D_FF = 5632
EPS = 1e-6

GLA_QK = GLA_HEADS * GLA_DK
GLA_V = GLA_HEADS * GLA_DV
SWA_Q = SWA_Q_HEADS * SWA_HEAD_DIM
SWA_KV = SWA_KV_HEADS * SWA_HEAD_DIM
SWA_GROUP = SWA_Q // SWA_KV_HEADS
IN_SIZES = (GLA_QK, GLA_QK, GLA_V, GLA_V, GLA_GATE_RANK, SWA_Q, SWA_KV, SWA_KV, SC_CH, SC_CH, SC_CH)
N_IN = sum(IN_SIZES)

LANES = 128
F32_ROWS = 8
BF16_ROWS = 16
MXU_TILE = 256
VMEM_BYTES = 64 * 1024 * 1024
VMEM_LIMIT = VMEM_BYTES * 7 // 8

Z_Q, Z_K, Z_V, Z_R = 0, 256, 512, 1024
Z_SQ, Z_CB, Z_CC, Z_CH = 1536, 2560, 3072, 3584
Z_SK, Z_SV, Z_LR = 4096, 4224, 4352
Z_COLS = 4480

GLA_CHUNK = 64
GLA_SUB = 16
GLA_NSUB = GLA_CHUNK // GLA_SUB
NEG_BIG = -1e30
SWA_STEP = 2 * SWA_WINDOW


class _Tiles:
    rmsnorm_rows = 512
    relayout_cols = 256
    cast_rows = 512
    mem_kv_cols = 512
    in_proj_cols = 7 * MXU_TILE
    xa_rows = 2048
    xa_out_rows = 512
    ffn_up_rows = 1024
    ffn_up_cols = 512
    ffn_down_rows = 512
    ffn_down_cols = 1024


def _params(*sem):
    return pltpu.CompilerParams(dimension_semantics=sem, vmem_limit_bytes=VMEM_LIMIT)


def _dot(a, b):
    return jnp.dot(a, b, preferred_element_type=F32)


def _dot_nt(a, b):
    return lax.dot_general(a, b, (((1,), (1,)), ((), ())), preferred_element_type=F32)


def _dot_tn(a, b):
    return lax.dot_general(a, b, (((0,), (0,)), ((), ())), preferred_element_type=F32)


def _silu(x):
    return x / (1.0 + jnp.exp(-x))


def _normalize(x, gain):
    return x * lax.rsqrt(jnp.mean(x * x, axis=-1, keepdims=True) + EPS) * gain


def _causal_conv3(u, before, c):
    rows = u.shape[0]
    ext = jnp.concatenate([before[BF16_ROWS - F32_ROWS:], u], axis=0)
    return (c[2:3] * u + c[1:2] * ext[F32_ROWS - 1:F32_ROWS - 1 + rows]
            + c[0:1] * ext[F32_ROWS - 2:F32_ROWS - 2 + rows])


def _layer_cols(layer, k, tn, col_of):
    return pl.BlockSpec((None, k, tn), lambda *ids: (layer, 0, col_of(*ids)))


def _rmsnorm_body(x_ref, g_ref, o_ref):
    o_ref[...] = _normalize(x_ref[...], g_ref[...]).astype(o_ref.dtype)


def _rmsnorm(x, gain):
    m, d = x.shape
    tm = min(_Tiles.rmsnorm_rows, m)
    return pl.pallas_call(
        _rmsnorm_body,
        grid=(m // tm,),
        in_specs=[pl.BlockSpec((tm, d), lambda i: (i, 0)),
                  pl.BlockSpec((1, d), lambda i: (0, 0))],
        out_specs=pl.BlockSpec((tm, d), lambda i: (i, 0)),
        out_shape=jax.ShapeDtypeStruct((m, d), x.dtype),
        compiler_params=_params("parallel"),
        name="rmsnorm",
    )(x, gain.reshape(1, d))


def _reorder_body(wt_ref, o_ref):
    o = np.cumsum((0,) + IN_SIZES)
    wt = wt_ref[...]
    pad = jnp.zeros((LANES - GLA_GATE_RANK, wt.shape[1]), wt.dtype)
    zt = jnp.concatenate(
        [wt[o[0]:o[4]], wt[o[5]:o[6]], wt[o[8]:o[11]], wt[o[6]:o[8]], wt[o[4]:o[5]], pad], axis=0)
    o_ref[...] = zt.T.astype(o_ref.dtype)


def _reorder_w_in(w_in_t):
    depth, n, d = w_in_t.shape
    tk = _Tiles.relayout_cols
    return pl.pallas_call(
        _reorder_body,
        grid=(depth, d // tk),
        in_specs=[pl.BlockSpec((None, n, tk), lambda l, i: (l, 0, i))],
        out_specs=pl.BlockSpec((None, tk, Z_COLS), lambda l, i: (l, i, 0)),
        out_shape=jax.ShapeDtypeStruct((depth, d, Z_COLS), BF16),
        compiler_params=_params("parallel", "parallel"),
        name="reorder_w_in",
    )(w_in_t)


def _cast_body(w_ref, o_ref):
    o_ref[...] = w_ref[...].astype(o_ref.dtype)


def _cast_bf16(w):
    depth, k, n = w.shape
    tk = _Tiles.cast_rows
    spec = pl.BlockSpec((None, tk, n), lambda l, i: (l, i, 0))
    return pl.pallas_call(
        _cast_body,
        grid=(depth, k // tk),
        in_specs=[spec],
        out_specs=spec,
        out_shape=jax.ShapeDtypeStruct(w.shape, BF16),
        compiler_params=_params("parallel", "parallel"),
        name="cast_bf16",
    )(w)


def _mem_kv_body(m_ref, g_ref, wk_ref, wv_ref, k_ref, v_ref, mn_ref):
    @pl.when(pl.program_id(1) == 0)
    def _():
        mn_ref[...] = _normalize(m_ref[...], g_ref[...]).astype(BF16)

    mn = mn_ref[...]
    k_ref[...] = _dot(mn, wk_ref[...].astype(BF16)).astype(k_ref.dtype)
    v_ref[...] = _dot(mn, wv_ref[...].astype(BF16)).astype(v_ref.dtype)


def _mem_kv(mem, gains, wk, wv):
    n_mem, d = mem.shape
    depth = wk.shape[0]
    tn = _Tiles.mem_kv_cols
    w_spec = pl.BlockSpec((None, d, tn), lambda l, j: (l, 0, j))
    o_spec = pl.BlockSpec((None, n_mem, tn), lambda l, j: (l, 0, j))
    out = jax.ShapeDtypeStruct((depth, n_mem, d), BF16)
    return pl.pallas_call(
        _mem_kv_body,
        grid=(depth, d // tn),
        in_specs=[pl.BlockSpec((n_mem, d), lambda l, j: (0, 0)),
                  pl.BlockSpec((None, 1, d), lambda l, j: (l, 0, 0)),
                  w_spec, w_spec],
        out_specs=[o_spec, o_spec],
        out_shape=[out, out],
        scratch_shapes=[pltpu.VMEM((n_mem, d), BF16)],
        compiler_params=_params("parallel", "arbitrary"),
        name="mem_kv",
    )(mem, gains.reshape(depth, 1, d), wk, wv)


def _ffn_down_body(a_ref, w_ref, h_ref, o_ref):
    o_ref[...] = h_ref[...] + _dot(a_ref[...], w_ref[...])


def _ffn_down(a, w, h):
    m, d = h.shape
    k = w.shape[0]
    tm, tn = min(_Tiles.ffn_down_rows, m), _Tiles.ffn_down_cols
    assert a.shape == (m, k)
    return pl.pallas_call(
        _ffn_down_body,
        grid=(d // tn, m // tm),
        in_specs=[pl.BlockSpec((tm, k), lambda j, i: (i, 0)),
                  pl.BlockSpec((k, tn), lambda j, i: (0, j)),
                  pl.BlockSpec((tm, tn), lambda j, i: (i, j))],
        out_specs=pl.BlockSpec((tm, tn), lambda j, i: (i, j)),
        out_shape=jax.ShapeDtypeStruct((m, d), F32),
        compiler_params=_params("parallel", "arbitrary"),
        name="ffn_down",
    )(a, w, h)


def _xa_out_body(a_ref, w_ref, h_ref, g_ref, o_ref, hn_ref, wb_ref):
    @pl.when(pl.program_id(0) == 0)
    def _():
        wb_ref[...] = w_ref[...].astype(BF16)

    h = h_ref[...] + _dot(a_ref[...], wb_ref[...])
    o_ref[...] = h
    hn_ref[...] = _normalize(h, g_ref[...]).astype(hn_ref.dtype)


def _xa_out_proj(a, w, layer, h, gain):
    m, d = h.shape
    k = w.shape[1]
    tm = min(_Tiles.xa_out_rows, m)
    rows = pl.BlockSpec((tm, d), lambda i: (i, 0))
    return pl.pallas_call(
        _xa_out_body,
        grid=(m // tm,),
        in_specs=[pl.BlockSpec((tm, k), lambda i: (i, 0)),
                  pl.BlockSpec((None, k, d), lambda i: (layer, 0, 0), pipeline_mode=pl.Buffered(1)),
                  rows,
                  pl.BlockSpec((1, d), lambda i: (0, 0))],
        out_specs=[rows, rows],
        out_shape=[jax.ShapeDtypeStruct((m, d), F32), jax.ShapeDtypeStruct((m, d), BF16)],
        scratch_shapes=[pltpu.VMEM((k, d), BF16)],
        compiler_params=_params("arbitrary"),
        name="xa_out_proj",
    )(a, w, h, gain.reshape(1, d))


def _xa_body(hn_ref, wq_ref, k_ref, v_ref, o_ref):
    q = _dot(hn_ref[...], wq_ref[...].astype(BF16)).astype(BF16)
    s = _dot_nt(q, k_ref[...]) * (XA_HEAD_DIM ** -0.5)
    m = jnp.max(s, axis=-1, keepdims=True)
    p = jnp.exp(s - m)
    denom = jnp.sum(p, axis=-1, keepdims=True)
    o = _dot(p.astype(BF16), v_ref[...]) / denom
    o_ref[...] = o.astype(o_ref.dtype)


def _cross_attention(hn, wq, layer, mem_k, mem_v):
    m, d = hn.shape
    tm = min(_Tiles.xa_rows, m)
    hd = XA_HEAD_DIM
    return pl.pallas_call(
        _xa_body,
        grid=(m // tm, XA_HEADS),
        in_specs=[pl.BlockSpec((tm, d), lambda i, j: (i, 0)),
                  _layer_cols(layer, d, hd, lambda i, j: j),
                  _layer_cols(layer, N_MEM, hd, lambda i, j: j),
                  _layer_cols(layer, N_MEM, hd, lambda i, j: j)],
        out_specs=pl.BlockSpec((tm, hd), lambda i, j: (i, j)),
        out_shape=jax.ShapeDtypeStruct((m, d), BF16),
        compiler_params=_params("parallel", "arbitrary"),
        name="cross_attention",
    )(hn, wq, mem_k, mem_v)


def _ffn_up_body(hn_ref, halo_ref, wg_ref, wv_ref, cg_ref, cv_ref, bg_ref, bv_ref, wd_ref, o_ref, wdb_ref):
    wdb_ref[...] = wd_ref[...].astype(wdb_ref.dtype)

    a = hn_ref[...]
    halo = jnp.where(pl.program_id(0) > 0, halo_ref[...], jnp.zeros_like(halo_ref))

    def conv_branch(w_ref, c_ref, b_ref):
        w = w_ref[...].astype(BF16)
        u = _dot(a, w)
        before = _dot(halo, w)
        return _causal_conv3(u, before, c_ref[...]) + b_ref[...]

    g = conv_branch(wg_ref, cg_ref, bg_ref)
    val = conv_branch(wv_ref, cv_ref, bv_ref)
    o_ref[...] = (_silu(g) * val).astype(o_ref.dtype)


def _ffn_up(hn, w_up, conv_w, conv_b, w_down, layer):
    m, d = hn.shape
    tm, tn = min(_Tiles.ffn_up_rows, m), _Tiles.ffn_up_cols
    nj = D_FF // tn
    halo_blocks = tm // BF16_ROWS
    steps = (m // tm) * nj
    cast_rows = D_FF // steps
    assert cast_rows * steps == D_FF and cast_rows % BF16_ROWS == 0

    def vec(rows, col_of):
        return pl.BlockSpec((None, rows, tn), lambda i, j: (layer, 0, col_of(j)))

    return pl.pallas_call(
        _ffn_up_body,
        grid=(m // tm, nj),
        in_specs=[pl.BlockSpec((tm, d), lambda i, j: (i, 0)),
                  pl.BlockSpec((BF16_ROWS, d), lambda i, j: (jnp.maximum(i * halo_blocks - 1, 0), 0)),
                  _layer_cols(layer, d, tn, lambda i, j: j),
                  _layer_cols(layer, d, tn, lambda i, j: nj + j),
                  vec(SC_WIDTH, lambda j: j), vec(SC_WIDTH, lambda j: nj + j),
                  vec(1, lambda j: j), vec(1, lambda j: nj + j),
                  pl.BlockSpec((None, cast_rows, d), lambda i, j: (layer, i * nj + j, 0))],
        out_specs=[pl.BlockSpec((tm, tn), lambda i, j: (i, j)),
                   pl.BlockSpec((cast_rows, d), lambda i, j: (i * nj + j, 0))],
        out_shape=[jax.ShapeDtypeStruct((m, D_FF), BF16), jax.ShapeDtypeStruct((D_FF, d), BF16)],
        compiler_params=_params("parallel", "arbitrary"),
        name="ffn_up",
    )(hn, hn, w_up, w_up, conv_w, conv_w,
      conv_b.reshape(conv_b.shape[0], 1, -1), conv_b.reshape(conv_b.shape[0], 1, -1), w_down)


def _mixer_body(sink_ref, cosc_ref, sinc_ref, cosp_ref, sinp_ref, cw_ref,
                wg_ref, bg_ref, gn_ref, sel_ref, ex_ref, bd_ref,
                hin_ref, gin_ref, wz_ref, hres_ref, wo_ref, gx_ref, o_ref, hn_ref,
                znew_ref, zcur_ref, kvprev_ref, halo_ref, mix_ref, st_ref, la_ref, *, last_tile):
    w = SWA_WINDOW
    s = pl.program_id(0)

    @pl.when(s == 0)
    def _():
        znew_ref[...] = jnp.zeros_like(znew_ref)
        zcur_ref[...] = jnp.zeros_like(zcur_ref)
        mix_ref[...] = jnp.zeros_like(mix_ref)

    @pl.when(s <= 1)
    def _():
        st_ref[...] = jnp.zeros_like(st_ref)

    kvprev_ref[...] = zcur_ref[w:, Z_SK:Z_SK + 2 * SWA_KV]
    halo_ref[...] = zcur_ref[SWA_STEP - BF16_ROWS:, Z_CC:Z_CC + 2 * SC_CH]
    zcur_ref[...] = znew_ref[...]

    h1 = hres_ref[...] + _dot(mix_ref[...], wo_ref[...])
    o_ref[...] = h1
    hn_ref[...] = _normalize(h1, gx_ref[...]).astype(hn_ref.dtype)

    hn_in = _normalize(hin_ref[...], gin_ref[...]).astype(BF16)
    for c0 in range(0, Z_COLS, _Tiles.in_proj_cols):
        c1 = min(c0 + _Tiles.in_proj_cols, Z_COLS)
        znew_ref[:, c0:c1] = _dot(hn_in, wz_ref[:, c0:c1]).astype(BF16)

    def zcols(off, width):
        return zcur_ref.at[:, off:off + width]

    q0_ref, q1_ref = zcols(Z_SQ, SWA_GROUP), zcols(Z_SQ + SWA_GROUP, SWA_GROUP)
    kc_ref, vc_ref = zcols(Z_SK, SWA_KV), zcols(Z_SV, SWA_KV)
    kp_ref, vp_ref = kvprev_ref.at[:, 0:SWA_KV], kvprev_ref.at[:, SWA_KV:2 * SWA_KV]
    cb_ref, cc_ref, ch_ref = zcols(Z_CB, SC_CH), zcols(Z_CC, SC_CH), zcols(Z_CH, SC_CH)
    hcc_ref, hch_ref = halo_ref.at[:, 0:SC_CH], halo_ref.at[:, SC_CH:2 * SC_CH]

    _gla_tile(zcols(Z_Q, GLA_QK), zcols(Z_K, GLA_QK), zcols(Z_V, GLA_V), zcols(Z_R, GLA_V), zcols(Z_LR, LANES),
              wg_ref, bg_ref, gn_ref, sel_ref, ex_ref, bd_ref, st_ref, la_ref, mix_ref)

    n = jnp.clip(s - 1, 0, last_tile)
    lane = lax.broadcasted_iota(jnp.int32, (1, LANES), 1)
    low_half = lane < SWA_HEAD_DIM
    first_rot = (lane & (SWA_HEAD_DIM - 1)) < SWA_HEAD_DIM // 2

    def rope(x, cos, sin):
        half = SWA_HEAD_DIM // 2
        swapped = jnp.where(first_rot, pltpu.roll(x, LANES - half, 1), pltpu.roll(x, half, 1))
        return x * cos + swapped * sin

    cosc, sinc = cosc_ref[...], sinc_ref[...]
    k_cur = rope(kc_ref[...].astype(F32), cosc, sinc)
    k_prev = rope(kp_ref[...].astype(F32), cosp_ref[...], sinp_ref[...])
    v_cur = vc_ref[...].astype(F32)
    v_prev = vp_ref[...].astype(F32)

    qi = lax.broadcasted_iota(jnp.int32, (w, 2 * w), 0)
    ki = lax.broadcasted_iota(jnp.int32, (w, 2 * w), 1)
    in_window = (ki > qi) & (ki <= qi + w)
    row2 = lax.broadcasted_iota(jnp.int32, (2 * w, 1), 0)

    pairs = SWA_GROUP // LANES
    for b in range(SWA_STEP // w):
        rows = slice(b * w, (b + 1) * w)
        if b == 0:
            k_all = jnp.concatenate([k_prev, k_cur[rows]], axis=0)
            v_all = jnp.concatenate([v_prev, v_cur[rows]], axis=0)
            allowed = in_window & ((n > 0) | (ki >= w))
        else:
            k_all = k_cur[(b - 1) * w:(b + 1) * w]
            v_all = v_cur[(b - 1) * w:(b + 1) * w]
            allowed = in_window
        bias = jnp.where(allowed, 0.0, NEG_BIG)
        bias2 = jnp.concatenate([bias, bias], axis=0)
        k_rot = pltpu.roll(k_all, SWA_HEAD_DIM, 1)
        v_rot = pltpu.roll(v_all, SWA_HEAD_DIM, 1)
        for g, q_ref in enumerate((q0_ref, q1_ref)):
            if g == 0:
                kg = jnp.where(low_half, k_all, k_rot).astype(BF16)
                vg = jnp.where(low_half, v_all, v_rot).astype(BF16)
            else:
                kg = jnp.where(low_half, k_rot, k_all).astype(BF16)
                vg = jnp.where(low_half, v_rot, v_all).astype(BF16)
            for p in range(pairs):
                qp = rope(q_ref[rows, p * LANES:(p + 1) * LANES].astype(F32), cosc[rows], sinc[rows])
                qp = qp * (SWA_HEAD_DIM ** -0.5)
                stack = jnp.concatenate([jnp.where(low_half, qp, 0.0), jnp.where(low_half, 0.0, qp)], axis=0)
                s = _dot_nt(stack.astype(BF16), kg) + bias2
                head = g * (SWA_Q_HEADS // SWA_KV_HEADS) + 2 * p
                sink = jnp.where(row2 < w, sink_ref[head], sink_ref[head + 1])
                mx = jnp.maximum(jnp.max(s, axis=-1, keepdims=True), sink)
                pr = jnp.exp(s - mx)
                denom = jnp.sum(pr, axis=-1, keepdims=True) + jnp.exp(sink - mx)
                o = _dot(pr.astype(BF16), vg) / denom
                col = GLA_V + (g * pairs + p) * LANES
                mix_ref[rows, col:col + LANES] = jnp.where(low_half, o[:w], o[w:]).astype(mix_ref.dtype)

    u = cc_ref[...].astype(F32) * ch_ref[...].astype(F32)
    before = hcc_ref[...].astype(F32) * hch_ref[...].astype(F32)
    before = jnp.where(n > 0, before, 0.0)
    y = _causal_conv3(u, before, cw_ref[...])
    mix_ref[:, GLA_V + SWA_Q:] = (cb_ref[...].astype(F32) * y).astype(mix_ref.dtype)


def _mixer(h, in_gain, w_z, w_out, layer, cos_t, sin_t, sinks, conv_w, w_gate, b_gate, gla_gain, next_gain):
    t, d = h.shape
    w, st = SWA_WINDOW, SWA_STEP
    nt = t // st
    assert d == GLA_V + SWA_Q + SC_CH
    sel, expand, bd_mask = _gla_constants()
    gain_t = jnp.tile(gla_gain.reshape(1, GLA_DV), (1, GLA_HEADS))
    b_gate = b_gate.reshape(1, GLA_QK)

    def whole(a):
        return pl.BlockSpec(a.shape, lambda i: (0, 0))

    def mixed(i):
        return jnp.clip(i - 1, 0, nt - 1)

    def rows(width, tile_of):
        return pl.BlockSpec((st, width), lambda i: (tile_of(i), 0))

    def resident(k, n):
        return pl.BlockSpec((None, k, n), lambda i: (layer, 0, 0), pipeline_mode=pl.Buffered(1))

    def projected(i):
        return jnp.clip(i - 2, 0, nt - 1)

    tab_c = rows(LANES, mixed)
    tab_p = pl.BlockSpec((w, LANES), lambda i: (jnp.maximum(mixed(i) * (st // w) - 1, 0), 0))
    return pl.pallas_call(
        functools.partial(_mixer_body, last_tile=nt - 1),
        grid=(nt + 2,),
        in_specs=[pl.BlockSpec(memory_space=pltpu.SMEM),
                  tab_c, tab_c, tab_p, tab_p,
                  pl.BlockSpec((SC_WIDTH, SC_CH), lambda i: (0, 0)),
                  whole(w_gate), whole(b_gate), whole(gain_t), whole(sel), whole(expand), whole(bd_mask),
                  rows(d, lambda i: jnp.minimum(i, nt - 1)), whole(in_gain.reshape(1, d)), resident(d, Z_COLS),
                  rows(d, projected), resident(d, d), whole(next_gain.reshape(1, d))],
        out_specs=[rows(d, projected), rows(d, projected)],
        out_shape=[jax.ShapeDtypeStruct((t, d), F32), jax.ShapeDtypeStruct((t, d), BF16)],
        scratch_shapes=[pltpu.VMEM((st, Z_COLS), BF16), pltpu.VMEM((st, Z_COLS), BF16),
                        pltpu.VMEM((w, 2 * SWA_KV), BF16), pltpu.VMEM((BF16_ROWS, 2 * SC_CH), BF16),
                        pltpu.VMEM((st, d), BF16),
                        pltpu.VMEM((GLA_V, GLA_QK), F32), pltpu.VMEM((st, GLA_QK), F32)],
        compiler_params=_params("arbitrary"),
        name="mixer",
    )(sinks, cos_t, sin_t, cos_t, sin_t, conv_w, w_gate, b_gate, gain_t, sel, expand, bd_mask,
      h, in_gain.reshape(1, d), w_z, h, w_out, next_gain.reshape(1, d))


def _gla_constants():
    c, r = GLA_CHUNK, GLA_SUB
    t = np.arange(c)[:, None]
    s = np.arange(c)[None, :]
    blk_t, blk_s = t // r, s // r
    mats = [
        (s <= t),
        (s > t),
        (blk_s == blk_t) & (s <= t) & (s > blk_t * r),
        (blk_s == blk_t) & (s > t),
    ]
    for i in range(1, GLA_NSUB):
        mats.append((blk_t < i) & (s > blk_t * r + r - 1) & (s <= i * r))
    sel = np.concatenate(mats, axis=0).astype(np.float32)
    d_head = np.arange(GLA_QK)[:, None] // GLA_DK
    e_head = np.arange(GLA_V)[None, :] // GLA_DV
    expand = (d_head == e_head).astype(np.float32)
    return jnp.asarray(sel, BF16), jnp.asarray(expand, BF16), jnp.asarray(expand.T, F32)


def _gla_tile(q_ref, k_ref, v_ref, r_ref, lr_ref, wg_ref, bg_ref, gn_ref, sel_ref, ex_ref, bd_ref,
              st_ref, la_ref, o_ref):
    c, r = GLA_CHUNK, GLA_SUB
    tc = q_ref.shape[0]

    pre = _dot(lr_ref[...], wg_ref[...]) + bg_ref[...]
    la_ref[...] = (jnp.minimum(pre, 0.0) - jnp.log(1.0 + jnp.exp(-jnp.abs(pre)))) * (1.0 / GLA_GATE_TAU)

    lane_qk = lax.broadcasted_iota(jnp.int32, (r, GLA_QK), 1)
    lane_v = lax.broadcasted_iota(jnp.int32, (r, GLA_V), 1)
    row_sub = lax.broadcasted_iota(jnp.int32, (r, GLA_QK), 0)
    row_up = lax.broadcasted_iota(jnp.int32, (r // 2, GLA_QK), 0) + r // 2
    col_chunk = lax.broadcasted_iota(jnp.int32, (c, c), 1)

    def head_rows(x, lane, width):
        return jnp.concatenate(
            [jnp.where((lane >= h * width) & (lane < (h + 1) * width), x, 0.0) for h in range(GLA_HEADS)],
            axis=0)

    def head_diag(x, lane, width):
        out = jnp.where(lane < width, x[0:r], 0.0)
        for h in range(1, GLA_HEADS):
            out = out + jnp.where((lane >= h * width) & (lane < (h + 1) * width), x[h * r:(h + 1) * r], 0.0)
        return out

    def chunk(ci):
        rows = pl.ds(ci * c, c)
        g = la_ref[rows, :]
        g_hi = g.astype(BF16)
        g_lo = (g - g_hi.astype(F32)).astype(BF16)
        cs = _dot(sel_ref[...], jnp.concatenate([g_hi, g_lo], axis=1))
        cs = cs[:, :GLA_QK] + cs[:, GLA_QK:]
        e_b = jnp.exp(cs[0:c])
        e_tail = jnp.exp(cs[c:2 * c])
        bw = cs[2 * c:3 * c]
        e_q = jnp.exp(bw)
        e_k = jnp.exp(cs[3 * c:4 * c])

        q = q_ref[rows, :].astype(F32) * (GLA_DK ** -0.5)
        k = k_ref[rows, :].astype(F32)
        v = v_ref[rows, :]
        vf = v.astype(F32)

        st = st_ref[...]
        o_inter = _dot_nt((q * e_b).astype(BF16), st.astype(BF16))
        upd = _dot_tn(v, (k * e_tail).astype(BF16))
        st_ref[...] = st * e_b[c - 1:c, :] + upd * bd_ref[...]

        qs = q * e_q
        ks = k * e_k
        atts = []
        for i in range(1, GLA_NSUB):
            ki = (ks * jnp.exp(cs[(3 + i) * c:(4 + i) * c])).astype(BF16)
            qst = head_rows(qs[i * r:(i + 1) * r], lane_qk, GLA_DK).astype(BF16)
            att = _dot_nt(qst, ki)
            atts.append(jnp.where(col_chunk < i * r, att, 0.0))
        ov = _dot(jnp.concatenate(atts, axis=0).astype(BF16), v)

        outs = []
        for i in range(GLA_NSUB):
            sub = slice(i * r, (i + 1) * r)
            qi, ki, bwi, vi = q[sub], k[sub], bw[sub], vf[sub]
            hr = r // 2
            upper = slice(i * r + hr, (i + 1) * r)
            q_up, bw_up = q[upper], bw[upper]
            ps = []
            for s in range(r):
                if s < hr:
                    e = jnp.exp(jnp.where(row_sub >= s, bwi - bwi[s:s + 1, :], NEG_BIG))
                    ps.append((qi * ki[s:s + 1, :]) * e)
                else:
                    e = jnp.exp(jnp.where(row_up >= s, bw_up - bwi[s:s + 1, :], NEG_BIG))
                    ps.append((q_up * ki[s:s + 1, :]) * e)
            rep = _dot(jnp.concatenate(ps, axis=0).astype(BF16), ex_ref[...])
            o_lo = o_inter[i * r:i * r + hr]
            o_hi = o_inter[i * r + hr:(i + 1) * r]
            for s in range(hr):
                o_lo = o_lo + rep[s * r:s * r + hr] * vi[s:s + 1, :]
                o_hi = o_hi + rep[s * r + hr:(s + 1) * r] * vi[s:s + 1, :]
            for s in range(hr, r):
                p0 = hr * r + (s - hr) * hr
                o_hi = o_hi + rep[p0:p0 + hr] * vi[s:s + 1, :]
            o_i = jnp.concatenate([o_lo, o_hi], axis=0)
            if i > 0:
                o_i = o_i + head_diag(ov[(i - 1) * GLA_HEADS * r:i * GLA_HEADS * r], lane_v, GLA_DV)
            outs.append(o_i)
        o = jnp.concatenate(outs, axis=0)

        normed = []
        for h in range(GLA_HEADS):
            oh = o[:, h * GLA_DV:(h + 1) * GLA_DV]
            normed.append(oh * lax.rsqrt(jnp.mean(oh * oh, axis=-1, keepdims=True) + EPS))
        y = jnp.concatenate(normed, axis=1) * gn_ref[...]
        o_ref[rows, 0:GLA_V] = (y * _silu(r_ref[rows, :].astype(F32))).astype(o_ref.dtype)
    for ci in range(tc // c):
        chunk(ci)


def _rope_tables(positions):
    inv = 1.0 / (ROPE_THETA ** (jnp.arange(0, SWA_HEAD_DIM, 2, dtype=F32) / SWA_HEAD_DIM))
    ang = positions.astype(F32)[:, None] * inv
    cos, sin = jnp.cos(ang), jnp.sin(ang)
    reps = LANES // SWA_HEAD_DIM
    return (jnp.tile(jnp.concatenate([cos, cos], axis=-1), (1, reps)),
            jnp.tile(jnp.concatenate([-sin, sin], axis=-1), (1, reps)))


def kernel(x, mem, positions, norm_mix, w_in, gla_w_gate, gla_b_gate, gla_norm, swa_sinks, sc_conv, w_out,
           norm_x, norm_mem, xa_wq, xa_wk, xa_wv, xa_wo, norm_ffn, ffn_w_up, ffn_conv, ffn_conv_b, ffn_w_down,
           norm_final):
    assert x.shape[0] == 1 and mem.shape[0] == 1 and x.shape[2] == D_MODEL
    assert x.shape[1] % SWA_STEP == 0 and w_in.shape[2] == N_IN and sc_conv.shape[1] == ffn_conv.shape[1] == SC_WIDTH
    depth = w_in.shape[0]
    h = x[0]
    m = mem[0]
    cos_t, sin_t = _rope_tables(positions[0])
    w_in_z = _reorder_w_in(jnp.swapaxes(w_in, 1, 2))
    w_out_b = _cast_bf16(w_out)
    w_gate = jnp.pad(gla_w_gate, ((0, 0), (0, LANES - GLA_GATE_RANK), (0, 0))).astype(BF16)
    mem_k, mem_v = _mem_kv(m, norm_mem, xa_wk, xa_wv)

    for l in range(depth):
        h, hn = _mixer(h, norm_mix[l], w_in_z, w_out_b, l, cos_t, sin_t, swa_sinks[l], sc_conv[l],
                       w_gate[l], gla_b_gate[l], gla_norm[l], norm_x[l])
        o_x = _cross_attention(hn, xa_wq, l, mem_k, mem_v)
        h, hn = _xa_out_proj(o_x, xa_wo, l, h, norm_ffn[l])
        act, w_down_b = _ffn_up(hn, ffn_w_up, ffn_conv, ffn_conv_b, ffn_w_down, l)
        h = _ffn_down(act, w_down_b, h)
    return _rmsnorm(h, norm_final)[None]
```

```python
import functools

import numpy as np
import jax
import jax.numpy as jnp
from jax import lax
from jax.experimental import pallas as pl
from jax.experimental.pallas import tpu as pltpu

F32 = jnp.float32
BF16 = jnp.bfloat16

D_MODEL = 2048
N_MEM = 256
XA_HEADS = 4
XA_HEAD_DIM = D_MODEL // XA_HEADS
GLA_HEADS = 4
GLA_DK = 64
GLA_DV = 128
GLA_GATE_RANK = 16
GLA_GATE_TAU = 16.0
SWA_Q_HEADS = 16
SWA_KV_HEADS = 2
SWA_HEAD_DIM = 64
SWA_WINDOW = 128
ROPE_THETA = 10000.0
SC_CH = 512
SC_WIDTH = 3
D_FF = 5632
EPS = 1e-6

GLA_QK = GLA_HEADS * GLA_DK
GLA_V = GLA_HEADS * GLA_DV
SWA_Q = SWA_Q_HEADS * SWA_HEAD_DIM
SWA_KV = SWA_KV_HEADS * SWA_HEAD_DIM
SWA_GROUP = SWA_Q // SWA_KV_HEADS
IN_SIZES = (GLA_QK, GLA_QK, GLA_V, GLA_V, GLA_GATE_RANK, SWA_Q, SWA_KV, SWA_KV, SC_CH, SC_CH, SC_CH)
N_IN = sum(IN_SIZES)

LANES = 128
F32_ROWS = 8
BF16_ROWS = 16
MXU_TILE = 256
VMEM_BYTES = 64 * 1024 * 1024
VMEM_LIMIT = VMEM_BYTES * 7 // 8

Z_Q, Z_K, Z_V, Z_R = 0, 256, 512, 1024
Z_SQ, Z_CB, Z_CC, Z_CH = 1536, 2560, 3072, 3584
Z_SK, Z_SV, Z_LR = 4096, 4224, 4352
Z_COLS = 4480

GLA_CHUNK = 64
GLA_SUB = 16
GLA_NSUB = GLA_CHUNK // GLA_SUB
NEG_BIG = -1e30
SWA_STEP = 2 * SWA_WINDOW


class _Tiles:
    rmsnorm_rows = 512
    relayout_cols = 256
    cast_rows = 512
    mem_kv_cols = 512
    in_proj_cols = 7 * MXU_TILE
    xa_rows = 2048
    xa_out_rows = 512
    ffn_up_rows = 1024
    ffn_up_cols = 512
    ffn_down_rows = 512
    ffn_down_cols = 1024


def _params(*sem):
    return pltpu.CompilerParams(dimension_semantics=sem, vmem_limit_bytes=VMEM_LIMIT)


def _dot(a, b):
    return jnp.dot(a, b, preferred_element_type=F32)


def _dot_nt(a, b):
    return lax.dot_general(a, b, (((1,), (1,)), ((), ())), preferred_element_type=F32)


def _dot_tn(a, b):
    return lax.dot_general(a, b, (((0,), (0,)), ((), ())), preferred_element_type=F32)


def _silu(x):
    return x / (1.0 + jnp.exp(-x))


def _normalize(x, gain):
    return x * lax.rsqrt(jnp.mean(x * x, axis=-1, keepdims=True) + EPS) * gain


def _causal_conv3(u, before, c):
    rows = u.shape[0]
    ext = jnp.concatenate([before[BF16_ROWS - F32_ROWS:], u], axis=0)
    return (c[2:3] * u + c[1:2] * ext[F32_ROWS - 1:F32_ROWS - 1 + rows]
            + c[0:1] * ext[F32_ROWS - 2:F32_ROWS - 2 + rows])


def _layer_cols(layer, k, tn, col_of):
    return pl.BlockSpec((None, k, tn), lambda *ids: (layer, 0, col_of(*ids)))


def _rmsnorm_body(x_ref, g_ref, o_ref):
    o_ref[...] = _normalize(x_ref[...], g_ref[...]).astype(o_ref.dtype)


def _rmsnorm(x, gain):
    m, d = x.shape
    tm = min(_Tiles.rmsnorm_rows, m)
    return pl.pallas_call(
        _rmsnorm_body,
        grid=(m // tm,),
        in_specs=[pl.BlockSpec((tm, d), lambda i: (i, 0)),
                  pl.BlockSpec((1, d), lambda i: (0, 0))],
        out_specs=pl.BlockSpec((tm, d), lambda i: (i, 0)),
        out_shape=jax.ShapeDtypeStruct((m, d), x.dtype),
        compiler_params=_params("parallel"),
        name="rmsnorm",
    )(x, gain.reshape(1, d))


def _reorder_body(wt_ref, o_ref):
    o = np.cumsum((0,) + IN_SIZES)
    wt = wt_ref[...]
    pad = jnp.zeros((LANES - GLA_GATE_RANK, wt.shape[1]), wt.dtype)
    zt = jnp.concatenate(
        [wt[o[0]:o[4]], wt[o[5]:o[6]], wt[o[8]:o[11]], wt[o[6]:o[8]], wt[o[4]:o[5]], pad], axis=0)
    o_ref[...] = zt.T.astype(o_ref.dtype)


def _reorder_w_in(w_in_t):
    depth, n, d = w_in_t.shape
    tk = _Tiles.relayout_cols
    return pl.pallas_call(
        _reorder_body,
        grid=(depth, d // tk),
        in_specs=[pl.BlockSpec((None, n, tk), lambda l, i: (l, 0, i))],
        out_specs=pl.BlockSpec((None, tk, Z_COLS), lambda l, i: (l, i, 0)),
        out_shape=jax.ShapeDtypeStruct((depth, d, Z_COLS), BF16),
        compiler_params=_params("parallel", "parallel"),
        name="reorder_w_in",
    )(w_in_t)


def _cast_body(w_ref, o_ref):
    o_ref[...] = w_ref[...].astype(o_ref.dtype)


def _cast_layer_bf16(w, layer):
    _, k, n = w.shape
    tk = _Tiles.cast_rows
    return pl.pallas_call(
        _cast_body,
        grid=(k // tk,),
        in_specs=[pl.BlockSpec((None, tk, n), lambda i: (layer, i, 0))],
        out_specs=pl.BlockSpec((tk, n), lambda i: (i, 0)),
        out_shape=jax.ShapeDtypeStruct((k, n), BF16),
        compiler_params=_params("parallel"),
        name="cast_bf16",
    )(w)


def _mem_kv_body(m_ref, g_ref, wk_ref, wv_ref, k_ref, v_ref, mn_ref):
    @pl.when(pl.program_id(1) == 0)
    def _():
        mn_ref[...] = _normalize(m_ref[...], g_ref[...]).astype(BF16)

    mn = mn_ref[...]
    k_ref[...] = _dot(mn, wk_ref[...].astype(BF16)).astype(k_ref.dtype)
    v_ref[...] = _dot(mn, wv_ref[...].astype(BF16)).astype(v_ref.dtype)


def _mem_kv(mem, gains, wk, wv):
    n_mem, d = mem.shape
    depth = wk.shape[0]
    tn = _Tiles.mem_kv_cols
    w_spec = pl.BlockSpec((None, d, tn), lambda l, j: (l, 0, j))
    o_spec = pl.BlockSpec((None, n_mem, tn), lambda l, j: (l, 0, j))
    out = jax.ShapeDtypeStruct((depth, n_mem, d), BF16)
    return pl.pallas_call(
        _mem_kv_body,
        grid=(depth, d // tn),
        in_specs=[pl.BlockSpec((n_mem, d), lambda l, j: (0, 0)),
                  pl.BlockSpec((None, 1, d), lambda l, j: (l, 0, 0)),
                  w_spec, w_spec],
        out_specs=[o_spec, o_spec],
        out_shape=[out, out],
        scratch_shapes=[pltpu.VMEM((n_mem, d), BF16)],
        compiler_params=_params("parallel", "arbitrary"),
        name="mem_kv",
    )(mem, gains.reshape(depth, 1, d), wk, wv)


def _ffn_down_body(a_ref, w_ref, h_ref, o_ref):
    o_ref[...] = h_ref[...] + _dot(a_ref[...], w_ref[...])


def _ffn_down(a, w, h):
    m, d = h.shape
    k = w.shape[0]
    tm, tn = min(_Tiles.ffn_down_rows, m), _Tiles.ffn_down_cols
    assert a.shape == (m, k)
    return pl.pallas_call(
        _ffn_down_body,
        grid=(d // tn, m // tm),
        in_specs=[pl.BlockSpec((tm, k), lambda j, i: (i, 0)),
                  pl.BlockSpec((k, tn), lambda j, i: (0, j)),
                  pl.BlockSpec((tm, tn), lambda j, i: (i, j))],
        out_specs=pl.BlockSpec((tm, tn), lambda j, i: (i, j)),
        out_shape=jax.ShapeDtypeStruct((m, d), F32),
        compiler_params=_params("parallel", "arbitrary"),
        name="ffn_down",
    )(a, w, h)


def _xa_out_body(a_ref, w_ref, h_ref, g_ref, o_ref, hn_ref):
    h = h_ref[...] + _dot(a_ref[...], w_ref[...])
    o_ref[...] = h
    hn_ref[...] = _normalize(h, g_ref[...]).astype(hn_ref.dtype)


def _xa_out_proj(a, w, h, gain):
    m, d = h.shape
    k = w.shape[0]
    tm = min(_Tiles.xa_out_rows, m)
    rows = pl.BlockSpec((tm, d), lambda i: (i, 0))
    return pl.pallas_call(
        _xa_out_body,
        grid=(m // tm,),
        in_specs=[pl.BlockSpec((tm, k), lambda i: (i, 0)),
                  pl.BlockSpec((k, d), lambda i: (0, 0), pipeline_mode=pl.Buffered(1)),
                  rows,
                  pl.BlockSpec((1, d), lambda i: (0, 0))],
        out_specs=[rows, rows],
        out_shape=[jax.ShapeDtypeStruct((m, d), F32), jax.ShapeDtypeStruct((m, d), BF16)],
        compiler_params=_params("parallel"),
        name="xa_out_proj",
    )(a, w, h, gain.reshape(1, d))


def _xa_body(*refs, ncast):
    hn_ref, wq_ref, k_ref, v_ref = refs[:4]
    src_refs = refs[4:4 + ncast]
    o_ref = refs[4 + ncast]
    dst_refs = refs[5 + ncast:]
    for src, dst in zip(src_refs, dst_refs):
        dst[...] = src[...].astype(dst.dtype)

    q = _dot(hn_ref[...], wq_ref[...].astype(BF16)).astype(BF16)
    s = _dot_nt(q, k_ref[...]) * (XA_HEAD_DIM ** -0.5)
    m = jnp.max(s, axis=-1, keepdims=True)
    p = jnp.exp(s - m)
    denom = jnp.sum(p, axis=-1, keepdims=True)
    o = _dot(p.astype(BF16), v_ref[...]) / denom
    o_ref[...] = o.astype(o_ref.dtype)


def _cross_attention(hn, wq, layer, mem_k, mem_v, casts):
    m, d = hn.shape
    tm = min(_Tiles.xa_rows, m)
    hd = XA_HEAD_DIM
    steps = (m // tm) * XA_HEADS
    cast_rows = d // steps
    assert cast_rows * steps == d and cast_rows % BF16_ROWS == 0 and all(w.shape[1:] == (d, d) for w, _ in casts)

    def share(l=None):
        block = (cast_rows, d) if l is None else (None, cast_rows, d)
        return pl.BlockSpec(block, lambda i, j: (i * XA_HEADS + j, 0) if l is None else (l, i * XA_HEADS + j, 0))

    res = pl.pallas_call(
        functools.partial(_xa_body, ncast=len(casts)),
        grid=(m // tm, XA_HEADS),
        in_specs=[pl.BlockSpec((tm, d), lambda i, j: (i, 0)),
                  _layer_cols(layer, d, hd, lambda i, j: j),
                  _layer_cols(layer, N_MEM, hd, lambda i, j: j),
                  _layer_cols(layer, N_MEM, hd, lambda i, j: j)] + [share(l) for _, l in casts],
        out_specs=[pl.BlockSpec((tm, hd), lambda i, j: (i, j))] + [share() for _ in casts],
        out_shape=[jax.ShapeDtypeStruct((m, d), BF16)] + [jax.ShapeDtypeStruct((d, d), BF16) for _ in casts],
        compiler_params=_params("parallel", "arbitrary"),
        name="cross_attention",
    )(hn, wq, mem_k, mem_v, *[w for w, _ in casts])
    return res[0], res[1:]


def _ffn_up_body(hn_ref, halo_ref, wg_ref, wv_ref, cg_ref, cv_ref, bg_ref, bv_ref, wd_ref, o_ref, wdb_ref):
    wdb_ref[...] = wd_ref[...].astype(wdb_ref.dtype)

    a = hn_ref[...]
    halo = jnp.where(pl.program_id(0) > 0, halo_ref[...], jnp.zeros_like(halo_ref))

    def conv_branch(w_ref, c_ref, b_ref):
        w = w_ref[...].astype(BF16)
        u = _dot(a, w)
        before = _dot(halo, w)
        return _causal_conv3(u, before, c_ref[...]) + b_ref[...]

    g = conv_branch(wg_ref, cg_ref, bg_ref)
    val = conv_branch(wv_ref, cv_ref, bv_ref)
    o_ref[...] = (_silu(g) * val).astype(o_ref.dtype)


def _ffn_up(hn, w_up, conv_w, conv_b, w_down, layer):
    m, d = hn.shape
    tm, tn = min(_Tiles.ffn_up_rows, m), _Tiles.ffn_up_cols
    nj = D_FF // tn
    halo_blocks = tm // BF16_ROWS
    steps = (m // tm) * nj
    cast_rows = D_FF // steps
    assert cast_rows * steps == D_FF and cast_rows % BF16_ROWS == 0

    def vec(rows, col_of):
        return pl.BlockSpec((None, rows, tn), lambda i, j: (layer, 0, col_of(j)))

    return pl.pallas_call(
        _ffn_up_body,
        grid=(m // tm, nj),
        in_specs=[pl.BlockSpec((tm, d), lambda i, j: (i, 0)),
                  pl.BlockSpec((BF16_ROWS, d), lambda i, j: (jnp.maximum(i * halo_blocks - 1, 0), 0)),
                  _layer_cols(layer, d, tn, lambda i, j: j),
                  _layer_cols(layer, d, tn, lambda i, j: nj + j),
                  vec(SC_WIDTH, lambda j: j), vec(SC_WIDTH, lambda j: nj + j),
                  vec(1, lambda j: j), vec(1, lambda j: nj + j),
                  pl.BlockSpec((None, cast_rows, d), lambda i, j: (layer, i * nj + j, 0))],
        out_specs=[pl.BlockSpec((tm, tn), lambda i, j: (i, j)),
                   pl.BlockSpec((cast_rows, d), lambda i, j: (i * nj + j, 0))],
        out_shape=[jax.ShapeDtypeStruct((m, D_FF), BF16), jax.ShapeDtypeStruct((D_FF, d), BF16)],
        compiler_params=_params("parallel", "arbitrary"),
        name="ffn_up",
    )(hn, hn, w_up, w_up, conv_w, conv_w,
      conv_b.reshape(conv_b.shape[0], 1, -1), conv_b.reshape(conv_b.shape[0], 1, -1), w_down)


def _mixer_body(sink_ref, cosc_ref, sinc_ref, cosp_ref, sinp_ref, cw_ref,
                wg_ref, bg_ref, gn_ref, sel_ref, ex_ref, bd_ref,
                hin_ref, gin_ref, wz_ref, hres_ref, wo_ref, gx_ref, o_ref, hn_ref,
                znew_ref, zcur_ref, kvprev_ref, halo_ref, mix_ref, st_ref, la_ref, *, last_tile):
    w = SWA_WINDOW
    s = pl.program_id(0)

    @pl.when(s == 0)
    def _():
        znew_ref[...] = jnp.zeros_like(znew_ref)
        zcur_ref[...] = jnp.zeros_like(zcur_ref)
        mix_ref[...] = jnp.zeros_like(mix_ref)

    @pl.when(s <= 1)
    def _():
        st_ref[...] = jnp.zeros_like(st_ref)

    kvprev_ref[...] = zcur_ref[w:, Z_SK:Z_SK + 2 * SWA_KV]
    halo_ref[...] = zcur_ref[SWA_STEP - BF16_ROWS:, Z_CC:Z_CC + 2 * SC_CH]
    zcur_ref[...] = znew_ref[...]

    h1 = hres_ref[...] + _dot(mix_ref[...], wo_ref[...])
    o_ref[...] = h1
    hn_ref[...] = _normalize(h1, gx_ref[...]).astype(hn_ref.dtype)

    hn_in = _normalize(hin_ref[...], gin_ref[...]).astype(BF16)
    for c0 in range(0, Z_COLS, _Tiles.in_proj_cols):
        c1 = min(c0 + _Tiles.in_proj_cols, Z_COLS)
        znew_ref[:, c0:c1] = _dot(hn_in, wz_ref[:, c0:c1]).astype(BF16)

    def zcols(off, width):
        return zcur_ref.at[:, off:off + width]

    q0_ref, q1_ref = zcols(Z_SQ, SWA_GROUP), zcols(Z_SQ + SWA_GROUP, SWA_GROUP)
    kc_ref, vc_ref = zcols(Z_SK, SWA_KV), zcols(Z_SV, SWA_KV)
    kp_ref, vp_ref = kvprev_ref.at[:, 0:SWA_KV], kvprev_ref.at[:, SWA_KV:2 * SWA_KV]
    cb_ref, cc_ref, ch_ref = zcols(Z_CB, SC_CH), zcols(Z_CC, SC_CH), zcols(Z_CH, SC_CH)
    hcc_ref, hch_ref = halo_ref.at[:, 0:SC_CH], halo_ref.at[:, SC_CH:2 * SC_CH]

    _gla_tile(zcols(Z_Q, GLA_QK), zcols(Z_K, GLA_QK), zcols(Z_V, GLA_V), zcols(Z_R, GLA_V), zcols(Z_LR, LANES),
              wg_ref, bg_ref, gn_ref, sel_ref, ex_ref, bd_ref, st_ref, la_ref, mix_ref)

    n = jnp.clip(s - 1, 0, last_tile)
    lane = lax.broadcasted_iota(jnp.int32, (1, LANES), 1)
    low_half = lane < SWA_HEAD_DIM
    first_rot = (lane & (SWA_HEAD_DIM - 1)) < SWA_HEAD_DIM // 2

    def rope(x, cos, sin):
        half = SWA_HEAD_DIM // 2
        swapped = jnp.where(first_rot, pltpu.roll(x, LANES - half, 1), pltpu.roll(x, half, 1))
        return x * cos + swapped * sin

    cosc, sinc = cosc_ref[...], sinc_ref[...]
    k_cur = rope(kc_ref[...].astype(F32), cosc, sinc)
    k_prev = rope(kp_ref[...].astype(F32), cosp_ref[...], sinp_ref[...])
    v_cur = vc_ref[...].astype(F32)
    v_prev = vp_ref[...].astype(F32)

    qi = lax.broadcasted_iota(jnp.int32, (w, 2 * w), 0)
    ki = lax.broadcasted_iota(jnp.int32, (w, 2 * w), 1)
    in_window = (ki > qi) & (ki <= qi + w)
    row2 = lax.broadcasted_iota(jnp.int32, (2 * w, 1), 0)

    pairs = SWA_GROUP // LANES
    for b in range(SWA_STEP // w):
        rows = slice(b * w, (b + 1) * w)
        if b == 0:
            k_all = jnp.concatenate([k_prev, k_cur[rows]], axis=0)
            v_all = jnp.concatenate([v_prev, v_cur[rows]], axis=0)
            allowed = in_window & ((n > 0) | (ki >= w))
        else:
            k_all = k_cur[(b - 1) * w:(b + 1) * w]
            v_all = v_cur[(b - 1) * w:(b + 1) * w]
            allowed = in_window
        bias = jnp.where(allowed, 0.0, NEG_BIG)
        bias2 = jnp.concatenate([bias, bias], axis=0)
        k_rot = pltpu.roll(k_all, SWA_HEAD_DIM, 1)
        v_rot = pltpu.roll(v_all, SWA_HEAD_DIM, 1)
        for g, q_ref in enumerate((q0_ref, q1_ref)):
            if g == 0:
                kg = jnp.where(low_half, k_all, k_rot).astype(BF16)
                vg = jnp.where(low_half, v_all, v_rot).astype(BF16)
            else:
                kg = jnp.where(low_half, k_rot, k_all).astype(BF16)
                vg = jnp.where(low_half, v_rot, v_all).astype(BF16)
            for p in range(pairs):
                qp = rope(q_ref[rows, p * LANES:(p + 1) * LANES].astype(F32), cosc[rows], sinc[rows])
                qp = qp * (SWA_HEAD_DIM ** -0.5)
                stack = jnp.concatenate([jnp.where(low_half, qp, 0.0), jnp.where(low_half, 0.0, qp)], axis=0)
                s = _dot_nt(stack.astype(BF16), kg) + bias2
                head = g * (SWA_Q_HEADS // SWA_KV_HEADS) + 2 * p
                sink = jnp.where(row2 < w, sink_ref[head], sink_ref[head + 1])
                mx = jnp.maximum(jnp.max(s, axis=-1, keepdims=True), sink)
                pr = jnp.exp(s - mx)
                denom = jnp.sum(pr, axis=-1, keepdims=True) + jnp.exp(sink - mx)
                o = _dot(pr.astype(BF16), vg) / denom
                col = GLA_V + (g * pairs + p) * LANES
                mix_ref[rows, col:col + LANES] = jnp.where(low_half, o[:w], o[w:]).astype(mix_ref.dtype)

    u = cc_ref[...].astype(F32) * ch_ref[...].astype(F32)
    before = hcc_ref[...].astype(F32) * hch_ref[...].astype(F32)
    before = jnp.where(n > 0, before, 0.0)
    y = _causal_conv3(u, before, cw_ref[...])
    mix_ref[:, GLA_V + SWA_Q:] = (cb_ref[...].astype(F32) * y).astype(mix_ref.dtype)


def _mixer(h, in_gain, w_z, w_out, layer, cos_t, sin_t, sinks, conv_w, w_gate, b_gate, gla_gain, next_gain):
    t, d = h.shape
    w, st = SWA_WINDOW, SWA_STEP
    nt = t // st
    assert d == GLA_V + SWA_Q + SC_CH
    sel, expand, bd_mask = _gla_constants()
    gain_t = jnp.tile(gla_gain.reshape(1, GLA_DV), (1, GLA_HEADS))
    b_gate = b_gate.reshape(1, GLA_QK)

    def whole(a):
        return pl.BlockSpec(a.shape, lambda i: (0, 0))

    def mixed(i):
        return jnp.clip(i - 1, 0, nt - 1)

    def rows(width, tile_of):
        return pl.BlockSpec((st, width), lambda i: (tile_of(i), 0))

    w_z_spec = pl.BlockSpec((None, d, Z_COLS), lambda i: (layer, 0, 0), pipeline_mode=pl.Buffered(1))
    w_out_spec = pl.BlockSpec((d, d), lambda i: (0, 0), pipeline_mode=pl.Buffered(1))

    def projected(i):
        return jnp.clip(i - 2, 0, nt - 1)

    tab_c = rows(LANES, mixed)
    tab_p = pl.BlockSpec((w, LANES), lambda i: (jnp.maximum(mixed(i) * (st // w) - 1, 0), 0))
    return pl.pallas_call(
        functools.partial(_mixer_body, last_tile=nt - 1),
        grid=(nt + 2,),
        in_specs=[pl.BlockSpec(memory_space=pltpu.SMEM),
                  tab_c, tab_c, tab_p, tab_p,
                  pl.BlockSpec((SC_WIDTH, SC_CH), lambda i: (0, 0)),
                  whole(w_gate), whole(b_gate), whole(gain_t), whole(sel), whole(expand), whole(bd_mask),
                  rows(d, lambda i: jnp.minimum(i, nt - 1)), whole(in_gain.reshape(1, d)), w_z_spec,
                  rows(d, projected), w_out_spec, whole(next_gain.reshape(1, d))],
        out_specs=[rows(d, projected), rows(d, projected)],
        out_shape=[jax.ShapeDtypeStruct((t, d), F32), jax.ShapeDtypeStruct((t, d), BF16)],
        scratch_shapes=[pltpu.VMEM((st, Z_COLS), BF16), pltpu.VMEM((st, Z_COLS), BF16),
                        pltpu.VMEM((w, 2 * SWA_KV), BF16), pltpu.VMEM((BF16_ROWS, 2 * SC_CH), BF16),
                        pltpu.VMEM((st, d), BF16),
                        pltpu.VMEM((GLA_V, GLA_QK), F32), pltpu.VMEM((st, GLA_QK), F32)],
        compiler_params=_params("arbitrary"),
        name="mixer",
    )(sinks, cos_t, sin_t, cos_t, sin_t, conv_w, w_gate, b_gate, gain_t, sel, expand, bd_mask,
      h, in_gain.reshape(1, d), w_z, h, w_out, next_gain.reshape(1, d))


def _gla_constants():
    c, r = GLA_CHUNK, GLA_SUB
    t = np.arange(c)[:, None]
    s = np.arange(c)[None, :]
    blk_t, blk_s = t // r, s // r
    mats = [
        (s <= t),
        (s > t),
        (blk_s == blk_t) & (s <= t) & (s > blk_t * r),
        (blk_s == blk_t) & (s > t),
    ]
    for i in range(1, GLA_NSUB):
        mats.append((blk_t < i) & (s > blk_t * r + r - 1) & (s <= i * r))
    sel = np.concatenate(mats, axis=0).astype(np.float32)
    d_head = np.arange(GLA_QK)[:, None] // GLA_DK
    e_head = np.arange(GLA_V)[None, :] // GLA_DV
    expand = (d_head == e_head).astype(np.float32)
    return jnp.asarray(sel, BF16), jnp.asarray(expand, BF16), jnp.asarray(expand.T, F32)


def _gla_tile(q_ref, k_ref, v_ref, r_ref, lr_ref, wg_ref, bg_ref, gn_ref, sel_ref, ex_ref, bd_ref,
              st_ref, la_ref, o_ref):
    c, r = GLA_CHUNK, GLA_SUB
    tc = q_ref.shape[0]

    pre = _dot(lr_ref[...], wg_ref[...]) + bg_ref[...]
    la_ref[...] = (jnp.minimum(pre, 0.0) - jnp.log(1.0 + jnp.exp(-jnp.abs(pre)))) * (1.0 / GLA_GATE_TAU)

    lane_qk = lax.broadcasted_iota(jnp.int32, (r, GLA_QK), 1)
    lane_v = lax.broadcasted_iota(jnp.int32, (r, GLA_V), 1)
    row_sub = lax.broadcasted_iota(jnp.int32, (r, GLA_QK), 0)
    row_up = lax.broadcasted_iota(jnp.int32, (r // 2, GLA_QK), 0) + r // 2
    col_chunk = lax.broadcasted_iota(jnp.int32, (c, c), 1)

    def head_rows(x, lane, width):
        return jnp.concatenate(
            [jnp.where((lane >= h * width) & (lane < (h + 1) * width), x, 0.0) for h in range(GLA_HEADS)],
            axis=0)

    def head_diag(x, lane, width):
        out = jnp.where(lane < width, x[0:r], 0.0)
        for h in range(1, GLA_HEADS):
            out = out + jnp.where((lane >= h * width) & (lane < (h + 1) * width), x[h * r:(h + 1) * r], 0.0)
        return out

    def chunk(ci):
        rows = pl.ds(ci * c, c)
        g = la_ref[rows, :]
        g_hi = g.astype(BF16)
        g_lo = (g - g_hi.astype(F32)).astype(BF16)
        cs = _dot(sel_ref[...], jnp.concatenate([g_hi, g_lo], axis=1))
        cs = cs[:, :GLA_QK] + cs[:, GLA_QK:]
        e_b = jnp.exp(cs[0:c])
        e_tail = jnp.exp(cs[c:2 * c])
        bw = cs[2 * c:3 * c]
        e_q = jnp.exp(bw)
        e_k = jnp.exp(cs[3 * c:4 * c])

        q = q_ref[rows, :].astype(F32) * (GLA_DK ** -0.5)
        k = k_ref[rows, :].astype(F32)
        v = v_ref[rows, :]
        vf = v.astype(F32)

        st = st_ref[...]
        o_inter = _dot_nt((q * e_b).astype(BF16), st.astype(BF16))
        upd = _dot_tn(v, (k * e_tail).astype(BF16))
        st_ref[...] = st * e_b[c - 1:c, :] + upd * bd_ref[...]

        qs = q * e_q
        ks = k * e_k
        atts = []
        for i in range(1, GLA_NSUB):
            ki = (ks * jnp.exp(cs[(3 + i) * c:(4 + i) * c])).astype(BF16)
            qst = head_rows(qs[i * r:(i + 1) * r], lane_qk, GLA_DK).astype(BF16)
            att = _dot_nt(qst, ki)
            atts.append(jnp.where(col_chunk < i * r, att, 0.0))
        ov = _dot(jnp.concatenate(atts, axis=0).astype(BF16), v)

        outs = []
        for i in range(GLA_NSUB):
            sub = slice(i * r, (i + 1) * r)
            qi, ki, bwi, vi = q[sub], k[sub], bw[sub], vf[sub]
            hr = r // 2
            upper = slice(i * r + hr, (i + 1) * r)
            q_up, bw_up = q[upper], bw[upper]
            ps = []
            for s in range(r):
                if s < hr:
                    e = jnp.exp(jnp.where(row_sub >= s, bwi - bwi[s:s + 1, :], NEG_BIG))
                    ps.append((qi * ki[s:s + 1, :]) * e)
                else:
                    e = jnp.exp(jnp.where(row_up >= s, bw_up - bwi[s:s + 1, :], NEG_BIG))
                    ps.append((q_up * ki[s:s + 1, :]) * e)
            rep = _dot(jnp.concatenate(ps, axis=0).astype(BF16), ex_ref[...])
            o_lo = o_inter[i * r:i * r + hr]
            o_hi = o_inter[i * r + hr:(i + 1) * r]
            for s in range(hr):
                o_lo = o_lo + rep[s * r:s * r + hr] * vi[s:s + 1, :]
                o_hi = o_hi + rep[s * r + hr:(s + 1) * r] * vi[s:s + 1, :]
            for s in range(hr, r):
                p0 = hr * r + (s - hr) * hr
                o_hi = o_hi + rep[p0:p0 + hr] * vi[s:s + 1, :]
            o_i = jnp.concatenate([o_lo, o_hi], axis=0)
            if i > 0:
                o_i = o_i + head_diag(ov[(i - 1) * GLA_HEADS * r:i * GLA_HEADS * r], lane_v, GLA_DV)
            outs.append(o_i)
        o = jnp.concatenate(outs, axis=0)

        normed = []
        for h in range(GLA_HEADS):
            oh = o[:, h * GLA_DV:(h + 1) * GLA_DV]
            normed.append(oh * lax.rsqrt(jnp.mean(oh * oh, axis=-1, keepdims=True) + EPS))
        y = jnp.concatenate(normed, axis=1) * gn_ref[...]
        o_ref[rows, 0:GLA_V] = (y * _silu(r_ref[rows, :].astype(F32))).astype(o_ref.dtype)
    for ci in range(tc // c):
        chunk(ci)


def _rope_tables(positions):
    inv = 1.0 / (ROPE_THETA ** (jnp.arange(0, SWA_HEAD_DIM, 2, dtype=F32) / SWA_HEAD_DIM))
    ang = positions.astype(F32)[:, None] * inv
    cos, sin = jnp.cos(ang), jnp.sin(ang)
    reps = LANES // SWA_HEAD_DIM
    return (jnp.tile(jnp.concatenate([cos, cos], axis=-1), (1, reps)),
            jnp.tile(jnp.concatenate([-sin, sin], axis=-1), (1, reps)))


def kernel(x, mem, positions, norm_mix, w_in, gla_w_gate, gla_b_gate, gla_norm, swa_sinks, sc_conv, w_out,
           norm_x, norm_mem, xa_wq, xa_wk, xa_wv, xa_wo, norm_ffn, ffn_w_up, ffn_conv, ffn_conv_b, ffn_w_down,
           norm_final):
    assert x.shape[0] == 1 and mem.shape[0] == 1 and x.shape[2] == D_MODEL
    assert x.shape[1] % SWA_STEP == 0 and w_in.shape[2] == N_IN and sc_conv.shape[1] == ffn_conv.shape[1] == SC_WIDTH
    depth = w_in.shape[0]
    h = x[0]
    m = mem[0]
    cos_t, sin_t = _rope_tables(positions[0])
    w_in_z = _reorder_w_in(jnp.swapaxes(w_in, 1, 2))
    w_out_b = _cast_layer_bf16(w_out, 0)
    w_gate = jnp.pad(gla_w_gate, ((0, 0), (0, LANES - GLA_GATE_RANK), (0, 0))).astype(BF16)
    mem_k, mem_v = _mem_kv(m, norm_mem, xa_wk, xa_wv)

    for l in range(depth):
        h, hn = _mixer(h, norm_mix[l], w_in_z, w_out_b, l, cos_t, sin_t, swa_sinks[l], sc_conv[l],
                       w_gate[l], gla_b_gate[l], gla_norm[l], norm_x[l])
        casts = [(xa_wo, l)] + ([(w_out, l + 1)] if l + 1 < depth else [])
        o_x, cast = _cross_attention(hn, xa_wq, l, mem_k, mem_v, casts)
        if l + 1 < depth:
            w_out_b = cast[1]
        h, hn = _xa_out_proj(o_x, cast[0], h, norm_ffn[l])
        act, w_down_b = _ffn_up(hn, ffn_w_up, ffn_conv, ffn_conv_b, ffn_w_down, l)
        h = _ffn_down(act, w_down_b, h)
    return _rmsnorm(h, norm_final)[None]
```
